```python
import math
import jax, jax.numpy as jnp
from jax import lax
import numpy as np

D_MODEL = 2048
BATCH = 4
SEQ = 2048
DEPTH = 4
DEC_BATCH = 128
DEC_SEQ = 1
PAST_LEN = 16384
PAGE_SIZE = 128

SSD_HEAD_DIM = 64
SSD_INNER = D_MODEL
SSD_HEADS = SSD_INNER // SSD_HEAD_DIM
SSD_GROUPS = 4
SSD_HPG = SSD_HEADS // SSD_GROUPS
SSD_STATE = 128
SSD_CHUNK = 128
CONV_W = 4
CONV_CH = SSD_INNER + 2 * SSD_GROUPS * SSD_STATE
SG_CHUNK = 128
SG_WIDTH = D_MODEL
SG_GROUPS = 8
SG_GDIM = SG_WIDTH // SG_GROUPS
N_EXP_GROUPS = 8
EXP_PER_GROUP = 8
N_EXPERTS = N_EXP_GROUPS * EXP_PER_GROUP
EXP_TOP_K = 2
D_EXPERT = D_MODEL // 4
MOE_BLOCK = 128
IN_COLS = SSD_INNER + CONV_CH + SSD_HEADS + 2 * SG_WIDTH + 2 * D_MODEL
DN_ALPHA = (2.0 * DEPTH) ** 0.25
DN_BETA = (8.0 * DEPTH) ** -0.25
NORM_EPS = 1e-5

kernel_name = "hybrid_ssd_chunkgmlp_hmoe_decode_step"


def _ln(x):
    xf = x.astype(jnp.float32)
    mu = jnp.mean(xf, -1, keepdims=True)
    var = jnp.mean(jnp.square(xf - mu), -1, keepdims=True)
    return (xf - mu) * lax.rsqrt(var + NORM_EPS)


def _causal_conv(full, w, b):
    out = lax.conv_general_dilated(full, w[:, None, :].astype(full.dtype), window_strides=(1,), padding='VALID',
                                   dimension_numbers=('NWC', 'WIO', 'NWC'), feature_group_count=full.shape[-1])
    return out + b.astype(full.dtype)


def _ssd_chunked(xh, dt, a, bm, cm):
    b, l = xh.shape[:2]
    nc = l // SSD_CHUNK
    ch = lambda t: t.reshape((b, nc, SSD_CHUNK) + t.shape[2:])
    xdt = ch(xh * dt[..., None])
    a_cum = jnp.cumsum(ch(dt * a), axis=2)
    bc, cc = ch(bm), ch(cm)
    causal = jnp.tril(jnp.ones((SSD_CHUNK, SSD_CHUNK), bool))
    act = jnp.moveaxis(a_cum, 2, -1)
    seg = act[..., :, None] - act[..., None, :]
    decay = jnp.where(causal, jnp.exp(jnp.where(causal, seg, 0.0)), 0.0)
    cb = jnp.einsum('bclgn,bcsgn->bcgls', cc, bc)
    y_diag = jnp.einsum('bcgls,bcgels,bcsgep->bclgep', cb, decay, xdt)
    decay_to_end = jnp.exp(a_cum[:, :, -1:] - a_cum)
    chunk_states = jnp.einsum('bclgn,bclge,bclgep->bcgepn', bc, decay_to_end, xdt)
    chunk_decay = jnp.exp(a_cum[:, :, -1])

    def step(hs, inp):
        s, d = inp
        return hs * d[..., None, None] + s, hs

    h0 = jnp.zeros((b, SSD_GROUPS, SSD_HPG, SSD_HEAD_DIM, SSD_STATE), jnp.float32)
    h_final, h_in = lax.scan(step, h0, (jnp.swapaxes(chunk_states, 0, 1), jnp.swapaxes(chunk_decay, 0, 1)))
    h_in = jnp.swapaxes(h_in, 0, 1)
    y_off = jnp.einsum('bclgn,bcgepn,bclge->bclgep', cc, h_in, jnp.exp(a_cum))
    y = (y_diag + y_off).reshape(b, l, SSD_GROUPS, SSD_HPG, SSD_HEAD_DIM)
    return y, h_final


def _ssd_recurrent(xh, dt, a, bm, cm, h0):
    def step(hs, inp):
        x_t, dt_t, b_t, c_t = inp
        hs = hs * jnp.exp(dt_t * a)[..., None, None] + jnp.einsum('bge,bgep,bgn->bgepn', dt_t, x_t, b_t)
        return hs, jnp.einsum('bgn,bgepn->bgep', c_t, hs)

    h_final, ys = lax.scan(step, h0, (jnp.swapaxes(xh, 0, 1), jnp.swapaxes(dt, 0, 1),
                                      jnp.swapaxes(bm, 0, 1), jnp.swapaxes(cm, 0, 1)))
    return jnp.swapaxes(ys, 0, 1), h_final


def _token_mixer(h, conv_prev, ssd_prev, w_in, conv_w, conv_b, dt_bias, a_log, d_skip, ssd_norm_w,
                 v_ln_g, v_ln_b, w_s, b_s, w_pa, w_pb, w_o):
    f32 = jnp.float32
    b, l, _ = h.shape
    o1 = SSD_INNER
    o2 = o1 + CONV_CH
    o3 = o2 + SSD_HEADS
    o4 = o3 + 2 * SG_WIDTH
    z, xbc, dt_raw, uv, gate_logits = jnp.split(h @ w_in, [o1, o2, o3, o4], axis=-1)
    if conv_prev is None:
        conv_prev = jnp.zeros((b, CONV_W - 1, CONV_CH), xbc.dtype)
    full = jnp.concatenate([conv_prev.astype(xbc.dtype), xbc], axis=1)
    new_conv = full[:, -(CONV_W - 1):]
    xbc = jax.nn.silu(_causal_conv(full, conv_w, conv_b))
    xs, bm, cm = jnp.split(xbc, [SSD_INNER, SSD_INNER + SSD_GROUPS * SSD_STATE], axis=-1)
    xh = xs.astype(f32).reshape(b, l, SSD_GROUPS, SSD_HPG, SSD_HEAD_DIM)
    bm = bm.astype(f32).reshape(b, l, SSD_GROUPS, SSD_STATE)
    cm = cm.astype(f32).reshape(b, l, SSD_GROUPS, SSD_STATE)
    dt = jax.nn.softplus(dt_raw.astype(f32) + dt_bias.astype(f32)).reshape(b, l, SSD_GROUPS, SSD_HPG)
    a = -jnp.exp(a_log.astype(f32)).reshape(SSD_GROUPS, SSD_HPG)
    if ssd_prev is None:
        y, h_ssd = _ssd_chunked(xh, dt, a, bm, cm)
    else:
        h0 = ssd_prev.astype(f32).reshape(b, SSD_GROUPS, SSD_HPG, SSD_HEAD_DIM, SSD_STATE)
        y, h_ssd = _ssd_recurrent(xh, dt, a, bm, cm, h0)
    y = y + d_skip.astype(f32).reshape(SSD_GROUPS, SSD_HPG, 1) * xh
    y = y.reshape(b, l, SSD_INNER) * jax.nn.silu(z.astype(f32))
    yg = y.reshape(b, l, SSD_GROUPS, SSD_INNER // SSD_GROUPS)
    yg = yg * lax.rsqrt(jnp.mean(jnp.square(yg), -1, keepdims=True) + NORM_EPS)
    y_a = (yg.reshape(b, l, SSD_INNER) * ssd_norm_w).astype(h.dtype)
    u, v = jnp.split(jax.nn.gelu(uv, approximate=False), 2, axis=-1)
    v = (_ln(v) * v_ln_g + v_ln_b).astype(h.dtype)
    lp = -(-l // SG_CHUNK) * SG_CHUNK
    vc = jnp.pad(v, ((0, 0), (0, lp - l), (0, 0))).reshape(b, lp // SG_CHUNK, SG_CHUNK, SG_GROUPS, SG_GDIM)
    s = jnp.einsum('gts,bcsgd->bctgd', jnp.tril(w_s), vc) + b_s.T[:, :, None]
    y_b = u * s.reshape(b, lp, SG_WIDTH)[:, :l].astype(h.dtype)
    g_a, g_b = jnp.split(jax.nn.sigmoid(gate_logits.astype(f32)).astype(h.dtype), 2, axis=-1)
    out = (g_a * (y_a @ w_pa) + g_b * (y_b @ w_pb)) @ w_o
    new_ssd = h_ssd.reshape(b, SSD_HEADS, SSD_HEAD_DIM, SSD_STATE).astype(h.dtype)
    return out, new_conv, new_ssd, v


def _moe(h, w_rg, b_rg, w_re, b_re, w_gate, w_up, w_down):
    f32 = jnp.float32
    b, l, d = h.shape
    t = h.reshape(-1, d)
    n_tok = t.shape[0]
    pg = jax.nn.softmax((t @ w_rg).astype(f32) + b_rg.astype(f32), axis=-1)
    grp = jnp.argmax(pg, axis=-1)
    p_grp = jnp.max(pg, axis=-1)
    le = ((t @ w_re).astype(f32) + b_re.astype(f32)).reshape(n_tok, N_EXP_GROUPS, EXP_PER_GROUP)
    le = jnp.take_along_axis(le, grp[:, None, None], axis=1)[:, 0]
    top_v, top_i = lax.top_k(le, EXP_TOP_K)
    wts = (jax.nn.softmax(top_v, axis=-1) * p_grp[:, None]).reshape(-1)
    eid = (grp[:, None] * EXP_PER_GROUP + top_i).reshape(-1)
    tok = jnp.repeat(jnp.arange(n_tok, dtype=jnp.int32), EXP_TOP_K)
    n_assign = n_tok * EXP_TOP_K
    order = jnp.argsort(eid, stable=True)
    eid_s, tok_s, w_srt = eid[order], tok[order], wts[order]
    counts = jnp.bincount(eid, length=N_EXPERTS)
    start = jnp.cumsum(counts) - counts
    padded = (counts + MOE_BLOCK - 1) // MOE_BLOCK * MOE_BLOCK
    pend = jnp.cumsum(padded)
    pstart = pend - padded
    dest = pstart[eid_s] + jnp.arange(n_assign) - start[eid_s]
    n_blocks = (n_assign + N_EXPERTS * (MOE_BLOCK - 1) + MOE_BLOCK - 1) // MOE_BLOCK
    buf_tok = jnp.zeros((n_blocks * MOE_BLOCK,), jnp.int32).at[dest].set(tok_s)
    blk_eid = jnp.minimum(jnp.searchsorted(pend, jnp.arange(n_blocks) * MOE_BLOCK, side='right'), N_EXPERTS - 1)
    xb = t[buf_tok].reshape(n_blocks, MOE_BLOCK, d)

    def expert_block(args):
        xblk, e = args
        hid = jax.nn.silu(xblk @ w_gate[e]) * (xblk @ w_up[e])
        return hid @ w_down[e]

    yb = lax.map(expert_block, (xb, blk_eid)).reshape(-1, d)
    y = jax.ops.segment_sum(yb[dest] * w_srt[:, None].astype(yb.dtype), tok_s, num_segments=n_tok)
    return y.reshape(b, l, d)


def _layer(x, c, conv_prev, ssd_prev, w_ada, b_ada, w_in, conv_w, conv_b, dt_bias, a_log, d_skip, ssd_norm_w,
           v_ln_g, v_ln_b, w_s, b_s, w_pa, w_pb, w_o, ln1_g, ln1_b, w_rg, b_rg, w_re, b_re,
           w_gate, w_up, w_down, ln2_g, ln2_b):
    f32 = jnp.float32
    mod = (jax.nn.silu(c) @ w_ada + b_ada)[:, None, :].astype(f32)
    sh_m, sc_m, g_m, sh_f, sc_f, g_f = jnp.split(mod, 6, axis=-1)
    h = (_ln(x) * (1.0 + sc_m) + sh_m).astype(x.dtype)
    mix, new_conv, new_ssd, v_rows = _token_mixer(h, conv_prev, ssd_prev, w_in, conv_w, conv_b, dt_bias, a_log,
                                                  d_skip, ssd_norm_w, v_ln_g, v_ln_b, w_s, b_s, w_pa, w_pb, w_o)
    x = (_ln(DN_ALPHA * x.astype(f32) + g_m * mix.astype(f32)) * ln1_g + ln1_b).astype(x.dtype)
    h = (_ln(x) * (1.0 + sc_f) + sh_f).astype(x.dtype)
    ffn = _moe(h, w_rg, b_rg, w_re, b_re, w_gate, w_up, w_down)
    x = (_ln(DN_ALPHA * x.astype(f32) + g_f * ffn.astype(f32)) * ln2_g + ln2_b).astype(x.dtype)
    return x, new_conv, new_ssd, v_rows


def setup_inputs(seed: int = 0) -> dict:
    key = jax.random.key(seed)
    ks = iter(jax.random.split(key, 40))
    f32 = jnp.float32

    def nrm(shape, scale):
        return jax.random.normal(next(ks), shape, f32) * scale

    dt0 = jnp.exp(jax.random.uniform(next(ks), (DEPTH, SSD_HEADS), f32) * (math.log(0.1) - math.log(0.001)) + math.log(0.001))
    dt_bias = dt0 + jnp.log(-jnp.expm1(-dt0))
    a_log = jnp.log(jax.random.uniform(next(ks), (DEPTH, SSD_HEADS), f32, minval=1.0, maxval=16.0))
    return {
        "x_prompt": nrm((BATCH, SEQ, D_MODEL), 1.0),
        "x_sample": nrm((DEC_BATCH, DEC_SEQ, D_MODEL), 1.0),
        "state_ssd": nrm((DEPTH, DEC_BATCH, SSD_HEADS, SSD_HEAD_DIM, SSD_STATE), 0.5),
        "state_conv": nrm((DEPTH, DEC_BATCH, CONV_W - 1, CONV_CH), 1.0),
        "c_prompt": nrm((BATCH, D_MODEL), 1.0),
        "c_sample": nrm((DEC_BATCH, D_MODEL), 1.0),
        "w_ada": nrm((DEPTH, D_MODEL, 6 * D_MODEL), 0.5 * D_MODEL ** -0.5),
        "b_ada": nrm((DEPTH, 6 * D_MODEL), 0.01),
        "w_in": nrm((DEPTH, D_MODEL, IN_COLS), D_MODEL ** -0.5),
        "conv_w": nrm((DEPTH, CONV_W, CONV_CH), CONV_W ** -0.5),
        "conv_b": nrm((DEPTH, CONV_CH), 0.01),
        "dt_bias": dt_bias,
        "a_log": a_log,
        "d_skip": 1.0 + nrm((DEPTH, SSD_HEADS), 0.01),
        "ssd_norm_w": 1.0 + nrm((DEPTH, SSD_INNER), 0.01),
        "v_ln_g": 1.0 + nrm((DEPTH, SG_WIDTH), 0.01),
        "v_ln_b": nrm((DEPTH, SG_WIDTH), 0.01),
        "w_s": nrm((DEPTH, SG_GROUPS, SG_CHUNK, SG_CHUNK), SG_CHUNK ** -0.5),
        "b_s": 1.0 + nrm((DEPTH, SG_GROUPS, SG_CHUNK), 0.01),
        "w_pa": nrm((DEPTH, SSD_INNER, D_MODEL), DN_BETA * SSD_INNER ** -0.5),
        "w_pb": nrm((DEPTH, SG_WIDTH, D_MODEL), DN_BETA * SG_WIDTH ** -0.5),
        "w_o": nrm((DEPTH, D_MODEL, D_MODEL), DN_BETA * D_MODEL ** -0.5),
        "ln1_g": 1.0 + nrm((DEPTH, D_MODEL), 0.01),
        "ln1_b": nrm((DEPTH, D_MODEL), 0.01),
        "w_rg": nrm((DEPTH, D_MODEL, N_EXP_GROUPS), D_MODEL ** -0.5),
        "b_rg": nrm((DEPTH, N_EXP_GROUPS), 0.01),
        "w_re": nrm((DEPTH, D_MODEL, N_EXPERTS), D_MODEL ** -0.5),
        "b_re": nrm((DEPTH, N_EXPERTS), 0.01),
        "w_gate": nrm((DEPTH, N_EXPERTS, D_MODEL, D_EXPERT), D_MODEL ** -0.5),
        "w_up": nrm((DEPTH, N_EXPERTS, D_MODEL, D_EXPERT), D_MODEL ** -0.5),
        "w_down": nrm((DEPTH, N_EXPERTS, D_EXPERT, D_MODEL), DN_BETA * D_EXPERT ** -0.5),
        "ln2_g": 1.0 + nrm((DEPTH, D_MODEL), 0.01),
        "ln2_b": nrm((DEPTH, D_MODEL), 0.01),
    }


def reference(x_prompt, x_sample, state_ssd, state_conv, c_prompt, c_sample, w_ada, b_ada, w_in, conv_w, conv_b,
              dt_bias, a_log, d_skip, ssd_norm_w, v_ln_g, v_ln_b, w_s, b_s, w_pa, w_pb, w_o, ln1_g, ln1_b,
              w_rg, b_rg, w_re, b_re, w_gate, w_up, w_down, ln2_g, ln2_b):
    xp, xs = x_prompt, x_sample
    ssd_p, conv_p, ssd_s, conv_s, v_s = [], [], [], [], []
    for i in range(DEPTH):
        lw = (w_ada[i], b_ada[i], w_in[i], conv_w[i], conv_b[i], dt_bias[i], a_log[i], d_skip[i], ssd_norm_w[i],
              v_ln_g[i], v_ln_b[i], w_s[i], b_s[i], w_pa[i], w_pb[i], w_o[i], ln1_g[i], ln1_b[i],
              w_rg[i], b_rg[i], w_re[i], b_re[i], w_gate[i], w_up[i], w_down[i], ln2_g[i], ln2_b[i])
        xp, cp, sp, _ = _layer(xp, c_prompt, None, None, *lw)
        xs, cs, ss, vs = _layer(xs, c_sample, state_conv[i], state_ssd[i], *lw)
        ssd_p.append(sp)
        conv_p.append(cp)
        ssd_s.append(ss)
        conv_s.append(cs)
        v_s.append(vs)
    return (xp, xs, jnp.stack(ssd_p), jnp.stack(conv_p), jnp.stack(ssd_s), jnp.stack(conv_s), jnp.stack(v_s))
```

```python
import functools
import math

import numpy as np
import jax
import jax.numpy as jnp
from jax import lax
from jax.experimental import pallas as pl
from jax.experimental.pallas import tpu as pltpu

F32 = jnp.float32
BF16 = jnp.bfloat16

D_MODEL = 2048
BATCH = 4
SEQ = 2048
DEPTH = 4
DEC_BATCH = 128
SSD_HEAD_DIM = 64
SSD_INNER = D_MODEL
SSD_HEADS = SSD_INNER // SSD_HEAD_DIM
SSD_GROUPS = 4
SSD_HPG = SSD_HEADS // SSD_GROUPS
SSD_STATE = 128
CONV_W = 4
BC_W = SSD_GROUPS * SSD_STATE
CONV_CH = SSD_INNER + 2 * BC_W
SG_CHUNK = 128
SG_WIDTH = D_MODEL
SG_GROUPS = 8
SG_GDIM = SG_WIDTH // SG_GROUPS
N_EXP_GROUPS = 8
EXP_PER_GROUP = 8
N_EXPERTS = 64
EXP_TOP_K = 2
D_EXPERT = D_MODEL // 4
DN_ALPHA = (2.0 * DEPTH) ** 0.25
NORM_EPS = 1e-5

T_PROMPT = BATCH * SEQ
T_ALL = T_PROMPT + DEC_BATCH
N_ASSIGN = T_ALL * EXP_TOP_K

LANES = 128
CHUNK = 128
VMEM_LIMIT = 56 * 1024 * 1024

SEG_Z = 0
SEG_XS = 2048
SEG_U = 4096
SEG_V = 6144
SEG_GA = 8192
SEG_GB = 10240
SEG_B = 12288
SEG_C = 12800
SEG_DT = 13312
PROJ_W = 13824
PROJ_TN = 512
ROUTE_W = LANES

SQRT_HALF = np.float32(np.sqrt(0.5))
NEG_BIG = -3.0e38


def _cparams(sem):
    return pltpu.CompilerParams(dimension_semantics=sem, vmem_limit_bytes=VMEM_LIMIT)


def _silu(x):
    return x * jax.nn.sigmoid(x)


def _gelu(x):
    return 0.5 * x * (1.0 + lax.erf(x * SQRT_HALF))


def _softplus(x):
    return jnp.maximum(x, 0.0) + jnp.log1p(jnp.exp(-jnp.abs(x)))


def _ln(x):
    mu = jnp.mean(x, axis=-1, keepdims=True)
    xc = x - mu
    var = jnp.mean(xc * xc, axis=-1, keepdims=True)
    return xc * lax.rsqrt(var + NORM_EPS)


ADA_TN = 1024


def _ada_kernel(c_ref, w_ref, b_ref, o_ref):
    a = _silu(c_ref[...]).astype(BF16)
    o_ref[...] = jnp.dot(a, w_ref[...].astype(BF16), preferred_element_type=F32) + b_ref[...]


def _ada_mod(c_all, w_ada, b_ada):
    rows = c_all.shape[0]
    n = 6 * D_MODEL
    return pl.pallas_call(
        _ada_kernel,
        grid=(DEPTH, n // ADA_TN),
        in_specs=[
            pl.BlockSpec((rows, D_MODEL), lambda l, j: (0, 0)),
            pl.BlockSpec((None, D_MODEL, ADA_TN), lambda l, j: (l, 0, j)),
            pl.BlockSpec((None, 1, ADA_TN), lambda l, j: (l, 0, j)),
        ],
        out_specs=pl.BlockSpec((None, rows, ADA_TN), lambda l, j: (l, 0, j)),
        out_shape=jax.ShapeDtypeStruct((DEPTH, rows, n), F32),
        compiler_params=_cparams(("arbitrary", "arbitrary")),
        name="ada_mod",
    )(c_all, w_ada, b_ada.reshape(DEPTH, 1, n))


MOD_SH_M, MOD_SC_M, MOD_G_M, MOD_SH_F, MOD_SC_F, MOD_G_F = range(6)


def _mod_spec(prompt, layer, k, tm):
    if prompt:
        return pl.BlockSpec((None, None, 1, D_MODEL), lambda i, *_: (layer, (i * tm) // SEQ, 0, k))
    return pl.BlockSpec((None, tm, D_MODEL), lambda i, *_: (layer, 0, k))


def _inproj_kernel(x_ref, sc_ref, sh_ref, w_ref, o_ref, h_ref):
    j = pl.program_id(1)

    @pl.when(j == 0)
    def _():
        h = _ln(x_ref[...]) * (1.0 + sc_ref[...]) + sh_ref[...]
        h_ref[...] = h.astype(BF16)

    acc = jnp.dot(h_ref[...], w_ref[...], preferred_element_type=F32)
    col = j * PROJ_TN

    @pl.when(col < SEG_XS)
    def _():
        o_ref[...] = _silu(acc)

    @pl.when((col >= SEG_U) & (col < SEG_GA))
    def _():
        o_ref[...] = _gelu(acc)

    @pl.when((col >= SEG_GA) & (col < SEG_B))
    def _():
        o_ref[...] = jax.nn.sigmoid(acc)

    @pl.when(((col >= SEG_XS) & (col < SEG_U)) | (col >= SEG_B))
    def _():
        o_ref[...] = acc


def _inproj(x, mod, w_pad, layer, prompt, tm):
    t = x.shape[0]
    return pl.pallas_call(
        _inproj_kernel,
        grid=(t // tm, PROJ_W // PROJ_TN),
        in_specs=[
            pl.BlockSpec((tm, D_MODEL), lambda i, j: (i, 0)),
            _mod_spec(prompt, layer, MOD_SC_M, tm),
            _mod_spec(prompt, layer, MOD_SH_M, tm),
            pl.BlockSpec((None, D_MODEL, PROJ_TN), lambda i, j: (layer, 0, j)),
        ],
        out_specs=pl.BlockSpec((tm, PROJ_TN), lambda i, j: (i, j)),
        out_shape=jax.ShapeDtypeStruct((t, PROJ_W), F32),
        scratch_shapes=[pltpu.VMEM((tm, D_MODEL), BF16)],
        compiler_params=_cparams(("arbitrary", "arbitrary")),
        name="inproj_p" if prompt else "inproj_s",
    )(x, mod, mod, w_pad)


def _pad_w_in(w_in):
    o_dt = SSD_INNER + CONV_CH
    o_uv = o_dt + SSD_HEADS
    pad = PROJ_W - SEG_DT - SSD_HEADS
    parts = [
        w_in[:, :, :2 * SSD_INNER],
        w_in[:, :, o_uv:],
        w_in[:, :, 2 * SSD_INNER:o_dt],
        w_in[:, :, o_dt:o_uv],
    ]
    w = jnp.concatenate([p.astype(BF16) for p in parts], axis=-1)
    return jnp.pad(w, ((0, 0), (0, 0), (0, pad)))


def _conv_chunk(ext_ref, raw, w_ref, b_ref):
    ext_ref[8:8 + CHUNK, :] = raw
    acc = b_ref[...] + w_ref[CONV_W - 1:CONV_W, :] * raw
    for k in range(CONV_W - 1):
        acc = acc + w_ref[k:k + 1, :] * ext_ref[5 + k:5 + k + CHUNK, :]
    ext_ref[5:8, :] = ext_ref[5 + CHUNK:8 + CHUNK, :]
    return acc


def _group_rmsnorm(y, norm_w):
    gw = SSD_INNER // SSD_GROUPS
    outs = []
    for g in range(SSD_GROUPS):
        yg = y[:, g * gw:(g + 1) * gw]
        ms = jnp.mean(yg * yg, axis=-1, keepdims=True)
        outs.append(yg * lax.rsqrt(ms + NORM_EPS))
    return jnp.concatenate(outs, axis=-1) * norm_w


def _mix_prompt_kernel(z_ref, xs_ref, b_ref, c_ref, dt_ref, u_ref, v_ref,
                       cwx_ref, cwb_ref, cwc_ref, cbx_ref, cbb_ref, cbc_ref,
                       dtb_ref, alog_ref, dskip_ref, nw_ref,
                       vg_ref, vb_ref, ws_ref, bst_ref,
                       ya_ref, yb_ref, st_ref, tail_ref,
                       ex_ref, eb_ref, ec_ref, st_t_ref, y_ref):
    c = pl.program_id(1)
    n_chunks = pl.num_programs(1)

    @pl.when(c == 0)
    def _():
        ex_ref[0:8, :] = jnp.zeros((8, SSD_INNER), F32)
        eb_ref[0:8, :] = jnp.zeros((8, BC_W), F32)
        ec_ref[0:8, :] = jnp.zeros((8, BC_W), F32)
        st_t_ref[...] = jnp.zeros_like(st_t_ref)

    raw_x = xs_ref[...]
    raw_b = b_ref[...]
    raw_c = c_ref[...]

    @pl.when(c == n_chunks - 1)
    def _():
        tail_ref[:, 0:SSD_INNER] = raw_x[CHUNK - 3:CHUNK, :]
        tail_ref[:, SSD_INNER:SSD_INNER + BC_W] = raw_b[CHUNK - 3:CHUNK, :]
        tail_ref[:, SSD_INNER + BC_W:CONV_CH] = raw_c[CHUNK - 3:CHUNK, :]

    xs = _silu(_conv_chunk(ex_ref, raw_x, cwx_ref, cbx_ref))
    bm = _silu(_conv_chunk(eb_ref, raw_b, cwb_ref, cbb_ref))
    cm = _silu(_conv_chunk(ec_ref, raw_c, cwc_ref, cbc_ref))

    dt = _softplus(dt_ref[...] + dtb_ref[...])
    a = -jnp.exp(alog_ref[...])
    d_a = dt * a
    rows = lax.broadcasted_iota(jnp.int32, (CHUNK, CHUNK), 0)
    cols = lax.broadcasted_iota(jnp.int32, (CHUNK, CHUNK), 1)
    causal = rows >= cols
    a_cum = jnp.dot(causal.astype(F32), d_a, preferred_element_type=F32,
                    precision=lax.Precision.HIGHEST)
    a_cum_t = a_cum.T
    dt_t = dt.T
    a_last_col = a_cum_t[:, CHUNK - 1:CHUNK]
    exp_a_cum = jnp.exp(a_cum)
    w_t = dt_t * jnp.exp(a_last_col - a_cum_t)
    ea_col = jnp.exp(a_last_col)
    lo = cols < SSD_HEAD_DIM

    for g in range(SSD_GROUPS):
        bg = bm[:, g * SSD_STATE:(g + 1) * SSD_STATE]
        cg = cm[:, g * SSD_STATE:(g + 1) * SSD_STATE]
        cb = lax.dot_general(cg.astype(BF16), bg.astype(BF16), (((1,), (1,)), ((), ())),
                             preferred_element_type=F32)
        bg_t = bg.T
        for q in range(SSD_HPG // 2):
            e0 = g * SSD_HPG + 2 * q
            sl = slice((e0 // 2) * LANES, (e0 // 2 + 1) * LANES)
            lhs = []
            upd = []
            for e in (e0, e0 + 1):
                seg = a_cum[:, e:e + 1] - a_cum_t[e:e + 1, :]
                dec = jnp.where(causal, jnp.exp(jnp.where(causal, seg, 0.0)), 0.0)
                lhs.append((cb * dec * dt_t[e:e + 1, :]).astype(BF16))
                lhs.append((cg * exp_a_cum[:, e:e + 1]).astype(BF16))
                upd.append((bg_t * w_t[e:e + 1, :]).astype(BF16))
            xs_p = xs[:, sl]
            st_p = st_t_ref[:, sl]
            xs_lo = jnp.where(lo, xs_p, 0.0).astype(BF16)
            xs_hi = jnp.where(lo, 0.0, xs_p).astype(BF16)
            st_lo = jnp.where(lo, st_p, 0.0).astype(BF16)
            st_hi = jnp.where(lo, 0.0, st_p).astype(BF16)
            y_ref[:, sl] = jnp.dot(jnp.concatenate(lhs, axis=1),
                                   jnp.concatenate([xs_lo, st_lo, xs_hi, st_hi], axis=0),
                                   preferred_element_type=F32)
            dec_p = jnp.where(lo, ea_col[e0:e0 + 1, :], ea_col[e0 + 1:e0 + 2, :])
            st_t_ref[:, sl] = st_p * dec_p + jnp.dot(jnp.concatenate(upd, axis=1),
                                                     jnp.concatenate([xs_lo, xs_hi], axis=0),
                                                     preferred_element_type=F32)

    y = (y_ref[...] + dskip_ref[...] * xs) * z_ref[...]
    ya_ref[...] = _group_rmsnorm(y, nw_ref[...]).astype(BF16)

    @pl.when(c == n_chunks - 1)
    def _():
        st_ref[...] = st_t_ref[...].T

    v = (_ln(v_ref[...]) * vg_ref[...] + vb_ref[...]).astype(BF16)
    for g in range(SG_GROUPS):
        sl = slice(g * SG_GDIM, (g + 1) * SG_GDIM)
        w = jnp.where(causal, ws_ref[g], 0.0).astype(BF16)
        s = jnp.dot(w, v[:, sl], preferred_element_type=F32) + bst_ref[:, g:g + 1]
        yb_ref[:, sl] = (u_ref[:, sl] * s).astype(BF16)


def _mix_prompt(proj, lw, layer):
    n_chunks = SEQ // CHUNK

    def pspec(off, width):
        return pl.BlockSpec((CHUNK, width), lambda b, c: (b * n_chunks + c, off // width))

    def full(arr):
        return pl.BlockSpec(arr.shape, lambda b, c: (0,) * arr.ndim)

    params = [lw["cwx"], lw["cwb"], lw["cwc"], lw["cbx"], lw["cbb"], lw["cbc"],
              lw["dtb"], lw["alog"], lw["dskip"], lw["nw"],
              lw["vg"], lw["vb"], lw["ws"], lw["bst"]]
    return pl.pallas_call(
        _mix_prompt_kernel,
        grid=(BATCH, n_chunks),
        in_specs=[pspec(SEG_Z, SSD_INNER), pspec(SEG_XS, SSD_INNER), pspec(SEG_B, BC_W), pspec(SEG_C, BC_W),
                  pspec(SEG_DT, LANES), pspec(SEG_U, SG_WIDTH), pspec(SEG_V, SG_WIDTH)]
        + [full(p) for p in params],
        out_specs=[
            pl.BlockSpec((CHUNK, SSD_INNER), lambda b, c: (b * n_chunks + c, 0)),
            pl.BlockSpec((CHUNK, SG_WIDTH), lambda b, c: (b * n_chunks + c, 0)),
            pl.BlockSpec((None, SSD_INNER, SSD_STATE), lambda b, c: (b, 0, 0)),
            pl.BlockSpec((None, CONV_W - 1, CONV_CH), lambda b, c: (b, 0, 0)),
        ],
        out_shape=[
            jax.ShapeDtypeStruct((T_PROMPT, SSD_INNER), BF16),
            jax.ShapeDtypeStruct((T_PROMPT, SG_WIDTH), BF16),
            jax.ShapeDtypeStruct((BATCH, SSD_INNER, SSD_STATE), F32),
            jax.ShapeDtypeStruct((BATCH, CONV_W - 1, CONV_CH), F32),
        ],
        scratch_shapes=[
            pltpu.VMEM((CHUNK + 8, SSD_INNER), F32),
            pltpu.VMEM((CHUNK + 8, BC_W), F32),
            pltpu.VMEM((CHUNK + 8, BC_W), F32),
            pltpu.VMEM((SSD_STATE, SSD_INNER), F32),
            pltpu.VMEM((CHUNK, SSD_INNER), F32),
        ],
        compiler_params=_cparams(("arbitrary", "arbitrary")),
        name="mix_prompt",
    )(proj, proj, proj, proj, proj, proj, proj, *params)


def _sprep_kernel(xs_ref, b_ref, c_ref, dt_ref, u_ref, v_ref, cs_ref,
                  cwx_ref, cwb_ref, cwc_ref, cbx_ref, cbb_ref, cbc_ref,
                  dtb_ref, alog_ref, vg_ref, vb_ref, ws0_ref, bs0_ref,
                  xa_ref, bm_ref, cm_ref, dec_ref, dtx_ref, yb_ref, vo_ref, ncs_ref):
    raw = (xs_ref[...], b_ref[...], c_ref[...])
    offs = (0, SSD_INNER, SSD_INNER + BC_W)
    widths = (SSD_INNER, BC_W, BC_W)
    cws = (cwx_ref, cwb_ref, cwc_ref)
    cbs = (cbx_ref, cbb_ref, cbc_ref)
    act = []
    for r, off, wd, cw, cbias in zip(raw, offs, widths, cws, cbs):
        acc = cbias[...] + cw[CONV_W - 1:CONV_W, :] * r
        for k in range(CONV_W - 1):
            acc = acc + cw[k:k + 1, :] * cs_ref[:, k * CONV_CH + off:k * CONV_CH + off + wd]
        act.append(_silu(acc))
        ncs_ref[:, (CONV_W - 2) * CONV_CH + off:(CONV_W - 2) * CONV_CH + off + wd] = r
    ncs_ref[:, 0:(CONV_W - 2) * CONV_CH] = cs_ref[:, CONV_CH:(CONV_W - 1) * CONV_CH]
    xa, bm, cm = act
    xa_ref[...] = xa
    bm_ref[...] = bm
    cm_ref[...] = cm

    dt = _softplus(dt_ref[...] + dtb_ref[...])
    a = -jnp.exp(alog_ref[...])
    dec_t = jnp.exp(dt * a).T
    dt_t = dt.T
    for p in range(SSD_HEADS // 2):
        xa_t = xa[:, p * LANES:(p + 1) * LANES].T
        for half in range(2):
            h = 2 * p + half
            rs = slice(h * SSD_HEAD_DIM, (h + 1) * SSD_HEAD_DIM)
            dec_ref[rs, :] = jnp.broadcast_to(dec_t[h:h + 1, :], (SSD_HEAD_DIM, DEC_BATCH))
            dtx_ref[rs, :] = xa_t[half * SSD_HEAD_DIM:(half + 1) * SSD_HEAD_DIM, :] * dt_t[h:h + 1, :]

    v = _ln(v_ref[...]) * vg_ref[...] + vb_ref[...]
    vo_ref[...] = v
    yb_ref[...] = (u_ref[...] * (ws0_ref[...] * v + bs0_ref[...])).astype(BF16)


def _sprep(proj, conv_state, lw, layer):
    def pspec(off, width):
        return pl.BlockSpec((DEC_BATCH, width), lambda i: (0, off // width))

    def full(arr):
        return pl.BlockSpec(arr.shape, lambda i: (0,) * arr.ndim)

    params = [lw["cwx"], lw["cwb"], lw["cwc"], lw["cbx"], lw["cbb"], lw["cbc"],
              lw["dtb"], lw["alog"], lw["vg"], lw["vb"], lw["ws0"], lw["bs0"]]
    cw = (CONV_W - 1) * CONV_CH
    outs = [
        ((DEC_BATCH, SSD_INNER), F32), ((DEC_BATCH, BC_W), F32), ((DEC_BATCH, BC_W), F32),
        ((SSD_INNER, DEC_BATCH), F32), ((SSD_INNER, DEC_BATCH), F32),
        ((DEC_BATCH, SG_WIDTH), BF16), ((DEC_BATCH, SG_WIDTH), F32), ((DEC_BATCH, cw), F32),
    ]
    return pl.pallas_call(
        _sprep_kernel,
        grid=(1,),
        in_specs=[pspec(SEG_XS, SSD_INNER), pspec(SEG_B, BC_W), pspec(SEG_C, BC_W), pspec(SEG_DT, LANES),
                  pspec(SEG_U, SG_WIDTH), pspec(SEG_V, SG_WIDTH),
                  pl.BlockSpec((None, DEC_BATCH, cw), lambda i: (layer, 0, 0))]
        + [full(p) for p in params],
        out_specs=[pl.BlockSpec(s, lambda i: (0, 0)) for s, _ in outs],
        out_shape=[jax.ShapeDtypeStruct(s, d) for s, d in outs],
        compiler_params=_cparams(("arbitrary",)),
        name="sample_prep",
    )(proj, proj, proj, proj, proj, proj, conv_state, *params)


SSTATE_BT = 8


def _sstate_kernel(st_ref, dec_ref, dtx_ref, bm_ref, cm_ref, xa_ref, z_ref, dskip_ref, nw_ref,
                   *rest):
    sto_ref, ya_ref, yt_ref, ycol_ref = rest[-4:]
    i = pl.program_id(0)
    lane = lax.broadcasted_iota(jnp.int32, (SSD_INNER, DEC_BATCH), 1)

    @pl.when(i == 0)
    def _():
        yt_ref[...] = jnp.zeros_like(yt_ref)

    base = i * SSTATE_BT
    shift = (DEC_BATCH - base) % DEC_BATCH
    dec_r = pltpu.roll(dec_ref[...], shift, 1)
    dtx_r = pltpu.roll(dtx_ref[...], shift, 1)
    ycol_ref[...] = jnp.zeros_like(ycol_ref)
    gr = SSD_HPG * SSD_HEAD_DIM
    for j in range(SSTATE_BT):
        b_row = bm_ref[pl.ds(base + j, 1), :]
        c_row = cm_ref[pl.ds(base + j, 1), :]
        for g in range(SSD_GROUPS):
            rs = slice(g * gr, (g + 1) * gr)
            ns = slice(g * SSD_STATE, (g + 1) * SSD_STATE)
            hn = st_ref[j, rs, :] * dec_r[rs, j:j + 1] + dtx_r[rs, j:j + 1] * b_row[:, ns]
            sto_ref[j, rs, :] = hn
            ycol_ref[rs, j:j + 1] = jnp.sum(hn * c_row[:, ns], axis=1, keepdims=True)
    put = (lane >= base) & (lane < base + SSTATE_BT)
    yt_ref[...] = jnp.where(put, pltpu.roll(ycol_ref[...], base, 1), yt_ref[...])

    @pl.when(i == pl.num_programs(0) - 1)
    def _():
        xa = xa_ref[...]
        cols = []
        for k in range(SSD_INNER // LANES):
            cols.append(yt_ref[k * LANES:(k + 1) * LANES, :].T)
        y = jnp.concatenate(cols, axis=1)
        y = (y + dskip_ref[...] * xa) * z_ref[...]
        ya_ref[...] = _group_rmsnorm(y, nw_ref[...]).astype(BF16)


def _sstate(state_all, stacked_prev, prep, proj, lw, layer):
    xa, bm, cm, dec, dtx = prep

    def full(arr):
        return pl.BlockSpec(arr.shape, lambda i: (0,) * arr.ndim)

    in_specs = [
        pl.BlockSpec((None, SSTATE_BT, SSD_INNER, SSD_STATE), lambda i: (layer, i, 0, 0)),
        full(dec), full(dtx), full(bm), full(cm), full(xa),
        pl.BlockSpec((DEC_BATCH, SSD_INNER), lambda i: (0, SEG_Z // SSD_INNER)),
        full(lw["dskip"]), full(lw["nw"]),
    ]
    args = [state_all, dec, dtx, bm, cm, xa, proj, lw["dskip"], lw["nw"]]
    aliases = {}
    if stacked_prev is not None:
        in_specs.append(pl.BlockSpec(memory_space=pl.ANY))
        args.append(stacked_prev)
        aliases = {len(args) - 1: 0}
    return pl.pallas_call(
        _sstate_kernel,
        grid=(DEC_BATCH // SSTATE_BT,),
        in_specs=in_specs,
        out_specs=[
            pl.BlockSpec((None, SSTATE_BT, SSD_INNER, SSD_STATE), lambda i: (layer, i, 0, 0)),
            pl.BlockSpec((DEC_BATCH, SSD_INNER), lambda i: (0, 0)),
        ],
        out_shape=[
            jax.ShapeDtypeStruct((DEPTH, DEC_BATCH, SSD_INNER, SSD_STATE), F32),
            jax.ShapeDtypeStruct((DEC_BATCH, SSD_INNER), BF16),
        ],
        scratch_shapes=[pltpu.VMEM((SSD_INNER, DEC_BATCH), F32), pltpu.VMEM((SSD_INNER, DEC_BATCH), F32)],
        input_output_aliases=aliases,
        compiler_params=_cparams(("arbitrary",)),
        name="sample_state",
    )(*args)


def _merge_kernel(ya_ref, yb_ref, ga_ref, gb_ref, x_ref, gm_ref, scf_ref, shf_ref,
                  wpa_ref, wpb_ref, wo_ref, l1g_ref, l1b_ref, wr_ref, br_ref, *rest):
    x1_ref, h2_ref, rt_ref = rest[-3:]
    t1 = jnp.dot(ya_ref[...], wpa_ref[...], preferred_element_type=F32)
    t2 = jnp.dot(yb_ref[...], wpb_ref[...], preferred_element_type=F32)
    m = (ga_ref[...] * t1 + gb_ref[...] * t2).astype(BF16)
    mix = jnp.dot(m, wo_ref[...], preferred_element_type=F32)
    x1 = _ln(DN_ALPHA * x_ref[...] + gm_ref[...] * mix) * l1g_ref[...] + l1b_ref[...]
    x1_ref[...] = x1
    h2 = _ln(x1) * (1.0 + scf_ref[...]) + shf_ref[...]
    h2_ref[...] = h2

    rl = jnp.dot(h2.astype(BF16), wr_ref[...], preferred_element_type=F32) + br_ref[...]
    lane = lax.broadcasted_iota(jnp.int32, rl.shape, 1)
    lane_f = lane.astype(F32)
    is_g = lane < N_EXP_GROUPS
    lg = jnp.where(is_g, rl, NEG_BIG)
    gmax = jnp.max(lg, axis=-1, keepdims=True)
    gidx = jnp.min(jnp.where(lg == gmax, lane_f, float(LANES)), axis=-1, keepdims=True)
    p_grp = 1.0 / jnp.sum(jnp.where(is_g, jnp.exp(lg - gmax), 0.0), axis=-1, keepdims=True)
    e_lane = lane - N_EXP_GROUPS
    in_grp = (e_lane >= 0) & (e_lane < N_EXPERTS) & (lax.shift_right_arithmetic(e_lane, 3).astype(F32) == gidx)
    le = jnp.where(in_grp, rl, NEG_BIG)
    m1 = jnp.max(le, axis=-1, keepdims=True)
    i1 = jnp.min(jnp.where(le == m1, lane_f, float(LANES)), axis=-1, keepdims=True)
    le2 = jnp.where(lane_f == i1, NEG_BIG, le)
    m2 = jnp.max(le2, axis=-1, keepdims=True)
    i2 = jnp.min(jnp.where(le2 == m2, lane_f, float(LANES)), axis=-1, keepdims=True)
    e2 = jnp.exp(m2 - m1)
    den = 1.0 + e2
    w1 = (1.0 / den) * p_grp
    w2 = (e2 / den) * p_grp
    rt_ref[...] = jnp.where(lane == 0, i1 - N_EXP_GROUPS,
                            jnp.where(lane == 1, i2 - N_EXP_GROUPS,
                                      jnp.where(lane == 2, w1, jnp.where(lane == 3, w2, 0.0))))


MERGE_TM = 256


def _merge(ya, yb, proj, x, mod, lw, layer, prompt, h2_prev):
    t = x.shape[0]
    tm = MERGE_TM if prompt else DEC_BATCH

    def const(shape):
        return pl.BlockSpec((None,) + shape, lambda i: (layer,) + (0,) * len(shape),
                            pipeline_mode=pl.Buffered(1))

    def gspec(off):
        return pl.BlockSpec((tm, D_MODEL), lambda i: (i, off // D_MODEL))

    in_specs = [
        pl.BlockSpec((tm, D_MODEL), lambda i: (i, 0)),
        pl.BlockSpec((tm, D_MODEL), lambda i: (i, 0)),
        gspec(SEG_GA), gspec(SEG_GB),
        pl.BlockSpec((tm, D_MODEL), lambda i: (i, 0)),
        _mod_spec(prompt, layer, MOD_G_M, tm),
        _mod_spec(prompt, layer, MOD_SC_F, tm),
        _mod_spec(prompt, layer, MOD_SH_F, tm),
        const((D_MODEL, D_MODEL)), const((D_MODEL, D_MODEL)), const((D_MODEL, D_MODEL)),
        const((1, D_MODEL)), const((1, D_MODEL)),
        const((D_MODEL, ROUTE_W)), const((1, ROUTE_W)),
    ]
    args = [ya, yb, proj, proj, x, mod, mod, mod, lw["w_pa"], lw["w_pb"], lw["w_o"],
            lw["ln1_g"], lw["ln1_b"], lw["w_r"], lw["b_r"]]
    aliases = {}
    h2_blk = 0
    if h2_prev is not None:
        in_specs.append(pl.BlockSpec(memory_space=pl.ANY))
        args.append(h2_prev)
        aliases = {len(args) - 1: 1}
        h2_blk = T_PROMPT // tm
    return pl.pallas_call(
        _merge_kernel,
        grid=(t // tm,),
        in_specs=in_specs,
        out_specs=[
            pl.BlockSpec((tm, D_MODEL), lambda i: (i, 0)),
            pl.BlockSpec((tm, D_MODEL), lambda i: (i + h2_blk, 0)),
            pl.BlockSpec((tm, ROUTE_W), lambda i: (i, 0)),
        ],
        out_shape=[
            jax.ShapeDtypeStruct((t, D_MODEL), F32),
            jax.ShapeDtypeStruct((T_ALL, D_MODEL), F32),
            jax.ShapeDtypeStruct((t, ROUTE_W), F32),
        ],
        input_output_aliases=aliases,
        compiler_params=_cparams(("arbitrary",)),
        name="merge_p" if prompt else "merge_s",
    )(*args)


MOE_ROWS = 128


def _moe_kernel(start_ref, count_ref, order_ref,
                h_hbm, wg_ref, wu_ref, wd_ref, out_hbm,
                xbuf, ybuf, wg_bf, wu_bf, wd_bf, sem_in, sem_out):
    e = pl.program_id(0)
    n = count_ref[e]
    s0 = start_ref[e]

    def gather_copy(r, idx):
        tok = order_ref[idx] // EXP_TOP_K
        return pltpu.make_async_copy(h_hbm.at[pl.ds(tok, 1), :], xbuf.at[pl.ds(r, 1), :], sem_in)

    def scatter_copy(r, idx):
        return pltpu.make_async_copy(ybuf.at[pl.ds(r, 1), :], out_hbm.at[pl.ds(order_ref[idx], 1), :], sem_out)

    @pl.when(n > 0)
    def _():
        wg_bf[...] = wg_ref[...].astype(BF16)
        wu_bf[...] = wu_ref[...].astype(BF16)
        wd_bf[...] = wd_ref[...].astype(BF16)
        last = s0 + n - 1

        def pass_body(p, carry):
            base = s0 + p * MOE_ROWS
            valid = jnp.minimum(n - p * MOE_ROWS, MOE_ROWS)

            def g_start(r, c):
                gather_copy(r, jnp.minimum(base + r, last)).start()
                return c

            def g_wait(r, c):
                gather_copy(r, s0).wait()
                return c

            lax.fori_loop(0, MOE_ROWS, g_start, 0)
            lax.fori_loop(0, MOE_ROWS, g_wait, 0)
            x = xbuf[...].astype(BF16)
            gate = jnp.dot(x, wg_bf[...], preferred_element_type=F32)
            up = jnp.dot(x, wu_bf[...], preferred_element_type=F32)
            hid = (_silu(gate) * up).astype(BF16)
            ybuf[...] = jnp.dot(hid, wd_bf[...], preferred_element_type=F32)

            def s_start(r, c):
                scatter_copy(r, base + r).start()
                return c

            def s_wait(r, c):
                scatter_copy(r, s0).wait()
                return c

            lax.fori_loop(0, valid, s_start, 0)
            lax.fori_loop(0, valid, s_wait, 0)
            return carry

        lax.fori_loop(0, pl.cdiv(n, MOE_ROWS), pass_body, 0)


def _moe(h2_all, start, count, order, w_gate, w_up, w_down, layer):
    grid_spec = pltpu.PrefetchScalarGridSpec(
        num_scalar_prefetch=3,
        grid=(N_EXPERTS,),
        in_specs=[
            pl.BlockSpec(memory_space=pl.ANY),
            pl.BlockSpec((None, None, D_MODEL, D_EXPERT), lambda e, *_: (layer, e, 0, 0)),
            pl.BlockSpec((None, None, D_MODEL, D_EXPERT), lambda e, *_: (layer, e, 0, 0)),
            pl.BlockSpec((None, None, D_EXPERT, D_MODEL), lambda e, *_: (layer, e, 0, 0)),
        ],
        out_specs=pl.BlockSpec(memory_space=pl.ANY),
        scratch_shapes=[
            pltpu.VMEM((MOE_ROWS, D_MODEL), F32),
            pltpu.VMEM((MOE_ROWS, D_MODEL), F32),
            pltpu.VMEM((D_MODEL, D_EXPERT), BF16),
            pltpu.VMEM((D_MODEL, D_EXPERT), BF16),
            pltpu.VMEM((D_EXPERT, D_MODEL), BF16),
            pltpu.SemaphoreType.DMA(()),
            pltpu.SemaphoreType.DMA(()),
        ],
    )
    return pl.pallas_call(
        _moe_kernel,
        grid_spec=grid_spec,
        out_shape=jax.ShapeDtypeStruct((N_ASSIGN, D_MODEL), F32),
        compiler_params=_cparams(("arbitrary",)),
        name="moe_experts",
    )(start, count, order, h2_all, w_gate, w_up, w_down)


def _final_kernel(x1_ref, o_ref, rt_ref, gf_ref, g_ref, b_ref, y_ref):
    rt = rt_ref[...]
    ffn = rt[:, 2:3] * o_ref[:, 0:D_MODEL] + rt[:, 3:4] * o_ref[:, D_MODEL:2 * D_MODEL]
    y_ref[...] = _ln(DN_ALPHA * x1_ref[...] + gf_ref[...] * ffn) * g_ref[...] + b_ref[...]


FINAL_TM = 512


def _final(x1, out2, route, mod, lw, layer, prompt):
    t = x1.shape[0]
    tm = FINAL_TM if prompt else DEC_BATCH
    blk0 = 0 if prompt else T_PROMPT // tm

    def const(shape):
        return pl.BlockSpec((None,) + shape, lambda i: (layer,) + (0,) * len(shape))

    return pl.pallas_call(
        _final_kernel,
        grid=(t // tm,),
        in_specs=[
            pl.BlockSpec((tm, D_MODEL), lambda i: (i, 0)),
            pl.BlockSpec((tm, EXP_TOP_K * D_MODEL), lambda i: (i + blk0, 0)),
            pl.BlockSpec((tm, ROUTE_W), lambda i: (i, 0)),
            _mod_spec(prompt, layer, MOD_G_F, tm),
            const((1, D_MODEL)), const((1, D_MODEL)),
        ],
        out_specs=pl.BlockSpec((tm, D_MODEL), lambda i: (i, 0)),
        out_shape=jax.ShapeDtypeStruct((t, D_MODEL), F32),
        compiler_params=_cparams(("arbitrary",)),
        name="final_p" if prompt else "final_s",
    )(x1, out2, route, mod, lw["ln2_g"], lw["ln2_b"])


def _layer_params(i, conv_w, conv_b, dt_bias, a_log, d_skip, ssd_norm_w, v_ln_g, v_ln_b, w_s, b_s,
                  stacked):
    pad_h = LANES - SSD_HEADS
    cw, cb = conv_w[i], conv_b[i][None, :]
    lw = dict(stacked)
    lw.update(
        cwx=cw[:, :SSD_INNER], cwb=cw[:, SSD_INNER:SSD_INNER + BC_W], cwc=cw[:, SSD_INNER + BC_W:],
        cbx=cb[:, :SSD_INNER], cbb=cb[:, SSD_INNER:SSD_INNER + BC_W], cbc=cb[:, SSD_INNER + BC_W:],
        dtb=jnp.pad(dt_bias[i], (0, pad_h))[None, :],
        alog=jnp.pad(a_log[i], (0, pad_h))[None, :],
        dskip=jnp.repeat(d_skip[i], SSD_HEAD_DIM)[None, :],
        nw=ssd_norm_w[i][None, :],
        vg=v_ln_g[i][None, :], vb=v_ln_b[i][None, :],
        ws=w_s[i], bst=b_s[i].T,
        ws0=jnp.repeat(w_s[i, :, 0, 0], SG_GDIM)[None, :],
        bs0=jnp.repeat(b_s[i, :, 0], SG_GDIM)[None, :],
    )
    return lw


def kernel(x_prompt, x_sample, state_ssd, state_conv, c_prompt, c_sample, w_ada, b_ada, w_in, conv_w, conv_b,
           dt_bias, a_log, d_skip, ssd_norm_w, v_ln_g, v_ln_b, w_s, b_s, w_pa, w_pb, w_o, ln1_g, ln1_b,
           w_rg, b_rg, w_re, b_re, w_gate, w_up, w_down, ln2_g, ln2_b):
    xp = x_prompt.reshape(T_PROMPT, D_MODEL)
    xs = x_sample.reshape(DEC_BATCH, D_MODEL)

    n_c = BATCH + DEC_BATCH
    c_all = jnp.pad(jnp.concatenate([c_sample, c_prompt], axis=0), ((0, (-n_c) % 8), (0, 0)))
    mod_s = _ada_mod(c_all, w_ada, b_ada)
    mod_p = mod_s[:, DEC_BATCH:n_c].reshape(DEPTH, BATCH, 1, 6 * D_MODEL)

    w_in_p = _pad_w_in(w_in)
    r_pad = ROUTE_W - N_EXP_GROUPS - N_EXPERTS
    stacked = dict(
        w_pa=w_pa.astype(BF16), w_pb=w_pb.astype(BF16), w_o=w_o.astype(BF16),
        ln1_g=ln1_g[:, None, :], ln1_b=ln1_b[:, None, :], ln2_g=ln2_g[:, None, :], ln2_b=ln2_b[:, None, :],
        w_r=jnp.pad(jnp.concatenate([w_rg, w_re], axis=-1), ((0, 0), (0, 0), (0, r_pad))).astype(BF16),
        b_r=jnp.pad(jnp.concatenate([b_rg, b_re], axis=-1), ((0, 0), (0, r_pad)))[:, None, :],
    )
    conv_state = state_conv.reshape(DEPTH, DEC_BATCH, (CONV_W - 1) * CONV_CH)
    state_all = state_ssd.reshape(DEPTH, DEC_BATCH, SSD_INNER, SSD_STATE)

    ssd_p, conv_p, conv_s, v_s = [], [], [], []
    ssd_s = None
    for i in range(DEPTH):
        lw = _layer_params(i, conv_w, conv_b, dt_bias, a_log, d_skip, ssd_norm_w, v_ln_g, v_ln_b, w_s, b_s,
                           stacked)
        proj_p = _inproj(xp, mod_p, w_in_p, i, True, 1024)
        proj_s = _inproj(xs, mod_s, w_in_p, i, False, DEC_BATCH)
        ya_p, yb_p, st_p, tail_p = _mix_prompt(proj_p, lw, i)
        xa, bm, cm, dec, dtx, yb_s, v_rows, ncs = _sprep(proj_s, conv_state, lw, i)
        ssd_s, ya_s = _sstate(state_all, ssd_s, (xa, bm, cm, dec, dtx), proj_s, lw, i)
        x1_p, h2_all, rt_p = _merge(ya_p, yb_p, proj_p, xp, mod_p, lw, i, True, None)
        x1_s, h2_all, rt_s = _merge(ya_s, yb_s, proj_s, xs, mod_s, lw, i, False, h2_all)
        route = jnp.concatenate([rt_p, rt_s], axis=0)
        eid = route[:, :EXP_TOP_K].astype(jnp.int32).reshape(-1)
        order = jnp.argsort(eid).astype(jnp.int32)
        count = jnp.sum((eid[:, None] == jnp.arange(N_EXPERTS, dtype=jnp.int32)[None, :]).astype(jnp.int32), axis=0)
        start = (jnp.cumsum(count) - count).astype(jnp.int32)
        out2 = _moe(h2_all, start, count, order, w_gate, w_up, w_down, i)
        out2 = out2.reshape(T_ALL, EXP_TOP_K * D_MODEL)
        xp = _final(x1_p, out2, rt_p, mod_p, lw, i, True)
        xs = _final(x1_s, out2, rt_s, mod_s, lw, i, False)

        ssd_p.append(st_p.reshape(BATCH, SSD_HEADS, SSD_HEAD_DIM, SSD_STATE))
        conv_p.append(tail_p)
        conv_s.append(ncs.reshape(DEC_BATCH, CONV_W - 1, CONV_CH))
        v_s.append(v_rows.reshape(DEC_BATCH, 1, SG_WIDTH))

    return (xp.reshape(BATCH, SEQ, D_MODEL), xs.reshape(DEC_BATCH, 1, D_MODEL),
            jnp.stack(ssd_p), jnp.stack(conv_p),
            ssd_s.reshape(DEPTH, DEC_BATCH, SSD_HEADS, SSD_HEAD_DIM, SSD_STATE),
            jnp.stack(conv_s), jnp.stack(v_s))
```

```python
import functools
import math

import numpy as np
import jax
import jax.numpy as jnp
from jax import lax
from jax.experimental import pallas as pl
from jax.experimental.pallas import tpu as pltpu

F32 = jnp.float32
BF16 = jnp.bfloat16

D_MODEL = 2048
BATCH = 4
SEQ = 2048
DEPTH = 4
DEC_BATCH = 128
SSD_HEAD_DIM = 64
SSD_INNER = D_MODEL
SSD_HEADS = SSD_INNER // SSD_HEAD_DIM
SSD_GROUPS = 4
SSD_HPG = SSD_HEADS // SSD_GROUPS
SSD_STATE = 128
CONV_W = 4
BC_W = SSD_GROUPS * SSD_STATE
CONV_CH = SSD_INNER + 2 * BC_W
SG_CHUNK = 128
SG_WIDTH = D_MODEL
SG_GROUPS = 8
SG_GDIM = SG_WIDTH // SG_GROUPS
N_EXP_GROUPS = 8
EXP_PER_GROUP = 8
N_EXPERTS = 64
EXP_TOP_K = 2
D_EXPERT = D_MODEL // 4
DN_ALPHA = (2.0 * DEPTH) ** 0.25
NORM_EPS = 1e-5

T_PROMPT = BATCH * SEQ
T_ALL = T_PROMPT + DEC_BATCH
N_ASSIGN = T_ALL * EXP_TOP_K

LANES = 128
CHUNK = 128
VMEM_LIMIT = 56 * 1024 * 1024

SEG_Z = 0
SEG_XS = 2048
SEG_U = 4096
SEG_V = 6144
SEG_GA = 8192
SEG_GB = 10240
SEG_B = 12288
SEG_C = 12800
SEG_DT = 13312
PROJ_W = 13824
PROJ_TN = 512
ROUTE_W = LANES

SQRT_HALF = np.float32(np.sqrt(0.5))
NEG_BIG = -3.0e38


def _cparams(sem):
    return pltpu.CompilerParams(dimension_semantics=sem, vmem_limit_bytes=VMEM_LIMIT)


def _silu(x):
    return x * jax.nn.sigmoid(x)


def _gelu(x):
    return 0.5 * x * (1.0 + lax.erf(x * SQRT_HALF))


def _softplus(x):
    return jnp.maximum(x, 0.0) + jnp.log1p(jnp.exp(-jnp.abs(x)))


def _ln(x):
    mu = jnp.mean(x, axis=-1, keepdims=True)
    xc = x - mu
    var = jnp.mean(xc * xc, axis=-1, keepdims=True)
    return xc * lax.rsqrt(var + NORM_EPS)


ADA_TN = 1024


def _ada_kernel(c_ref, w_ref, b_ref, o_ref):
    a = _silu(c_ref[...]).astype(BF16)
    o_ref[...] = jnp.dot(a, w_ref[...].astype(BF16), preferred_element_type=F32) + b_ref[...]


def _ada_mod(c_all, w_ada, b_ada):
    rows = c_all.shape[0]
    n = 6 * D_MODEL
    return pl.pallas_call(
        _ada_kernel,
        grid=(DEPTH, n // ADA_TN),
        in_specs=[
            pl.BlockSpec((rows, D_MODEL), lambda l, j: (0, 0)),
            pl.BlockSpec((None, D_MODEL, ADA_TN), lambda l, j: (l, 0, j)),
            pl.BlockSpec((None, 1, ADA_TN), lambda l, j: (l, 0, j)),
        ],
        out_specs=pl.BlockSpec((None, rows, ADA_TN), lambda l, j: (l, 0, j)),
        out_shape=jax.ShapeDtypeStruct((DEPTH, rows, n), F32),
        compiler_params=_cparams(("arbitrary", "arbitrary")),
        name="ada_mod",
    )(c_all, w_ada, b_ada.reshape(DEPTH, 1, n))


MOD_SH_M, MOD_SC_M, MOD_G_M, MOD_SH_F, MOD_SC_F, MOD_G_F = range(6)


def _mod_spec(prompt, layer, k, tm):
    if prompt:
        return pl.BlockSpec((None, None, 1, D_MODEL), lambda i, *_: (layer, (i * tm) // SEQ, 0, k))
    return pl.BlockSpec((None, tm, D_MODEL), lambda i, *_: (layer, 0, k))


INPROJ_RC = 256


def _inproj_kernel(x_ref, sc_ref, sh_ref, w_ref, o_ref, h_ref):
    j = pl.program_id(1)

    @pl.when(j == 0)
    def _():
        h = _ln(x_ref[...]) * (1.0 + sc_ref[...]) + sh_ref[...]
        h_ref[...] = h.astype(BF16)

    col = j * PROJ_TN
    tm = h_ref.shape[0]
    rc = min(tm, INPROJ_RC)

    def emit(act):
        for r in range(tm // rc):
            rows = slice(r * rc, (r + 1) * rc)
            o_ref[rows, :] = act(jnp.dot(h_ref[rows, :], w_ref[...], preferred_element_type=F32))

    @pl.when(col < SEG_XS)
    def _():
        emit(_silu)

    @pl.when((col >= SEG_U) & (col < SEG_GA))
    def _():
        emit(_gelu)

    @pl.when((col >= SEG_GA) & (col < SEG_B))
    def _():
        emit(jax.nn.sigmoid)

    @pl.when(((col >= SEG_XS) & (col < SEG_U)) | (col >= SEG_B))
    def _():
        emit(lambda a: a)


def _inproj(x, mod, w_pad, layer, prompt, tm):
    t = x.shape[0]
    return pl.pallas_call(
        _inproj_kernel,
        grid=(t // tm, PROJ_W // PROJ_TN),
        in_specs=[
            pl.BlockSpec((tm, D_MODEL), lambda i, j: (i, 0)),
            _mod_spec(prompt, layer, MOD_SC_M, tm),
            _mod_spec(prompt, layer, MOD_SH_M, tm),
            pl.BlockSpec((None, D_MODEL, PROJ_TN), lambda i, j: (layer, 0, j)),
        ],
        out_specs=pl.BlockSpec((tm, PROJ_TN), lambda i, j: (i, j)),
        out_shape=jax.ShapeDtypeStruct((t, PROJ_W), F32),
        scratch_shapes=[pltpu.VMEM((tm, D_MODEL), BF16)],
        compiler_params=_cparams(("arbitrary", "arbitrary")),
        name="inproj_p" if prompt else "inproj_s",
    )(x, mod, mod, w_pad)


REPACK_NB = PROJ_TN // LANES + 1
IN_COLS = SSD_INNER + CONV_CH + SSD_HEADS + 2 * SG_WIDTH + 2 * D_MODEL
SRC_DT = SSD_INNER + CONV_CH


def _repack_kernel(*refs):
    o_ref = refs[-1]
    m = pl.program_id(1)
    cat = jnp.concatenate([r[...] for r in refs[:-1]], axis=1)
    shifted = (m >= SEG_U // PROJ_TN) & (m < SEG_B // PROJ_TN)
    is_dt = m == SEG_DT // PROJ_TN

    @pl.when(shifted)
    def _():
        o_ref[...] = cat[:, SSD_HEADS:SSD_HEADS + PROJ_TN].astype(BF16)

    @pl.when(is_dt)
    def _():
        lane = lax.broadcasted_iota(jnp.int32, (D_MODEL, PROJ_TN), 1)
        o_ref[...] = jnp.where(lane < SSD_HEADS, cat[:, :PROJ_TN], 0.0).astype(BF16)

    @pl.when(jnp.logical_not(shifted | is_dt))
    def _():
        o_ref[...] = cat[:, :PROJ_TN].astype(BF16)


def _pad_w_in(w_in):
    n_src_blocks = pl.cdiv(IN_COLS, LANES)
    per_tile = PROJ_TN // LANES

    def src_block(m):
        t_u, t_b, t_dt = SEG_U // PROJ_TN, SEG_B // PROJ_TN, SEG_DT // PROJ_TN
        return jnp.where(m < t_u, m * per_tile,
                         jnp.where(m < t_b, SRC_DT // LANES + (m - t_u) * per_tile,
                                   jnp.where(m < t_dt, 2 * SSD_INNER // LANES + (m - t_b) * per_tile,
                                             SRC_DT // LANES)))

    def spec(i):
        return pl.BlockSpec((None, D_MODEL, LANES),
                            lambda l, m: (l, 0, jnp.minimum(src_block(m) + i, n_src_blocks - 1)))

    return pl.pallas_call(
        _repack_kernel,
        grid=(DEPTH, PROJ_W // PROJ_TN),
        in_specs=[spec(i) for i in range(REPACK_NB)],
        out_specs=pl.BlockSpec((None, D_MODEL, PROJ_TN), lambda l, m: (l, 0, m)),
        out_shape=jax.ShapeDtypeStruct((DEPTH, D_MODEL, PROJ_W), BF16),
        compiler_params=_cparams(("arbitrary", "arbitrary")),
        name="repack_w_in",
    )(*([w_in] * REPACK_NB))


def _conv_chunk(ext_ref, raw, w_ref, b_ref):
    ext_ref[8:8 + CHUNK, :] = raw
    acc = b_ref[...] + w_ref[CONV_W - 1:CONV_W, :] * raw
    for k in range(CONV_W - 1):
        acc = acc + w_ref[k:k + 1, :] * ext_ref[5 + k:5 + k + CHUNK, :]
    ext_ref[5:8, :] = ext_ref[5 + CHUNK:8 + CHUNK, :]
    return acc


def _group_rmsnorm(y, norm_w):
    gw = SSD_INNER // SSD_GROUPS
    outs = []
    for g in range(SSD_GROUPS):
        yg = y[:, g * gw:(g + 1) * gw]
        ms = jnp.mean(yg * yg, axis=-1, keepdims=True)
        outs.append(yg * lax.rsqrt(ms + NORM_EPS))
    return jnp.concatenate(outs, axis=-1) * norm_w


def _mix_prompt_kernel(z_ref, xs_ref, b_ref, c_ref, dt_ref, u_ref, v_ref,
                       cwx_ref, cwb_ref, cwc_ref, cbx_ref, cbb_ref, cbc_ref,
                       dtb_ref, alog_ref, dskip_ref, nw_ref,
                       vg_ref, vb_ref, ws_ref, bst_ref,
                       ya_ref, yb_ref, st_ref, tail_ref,
                       ex_ref, eb_ref, ec_ref, st_t_ref, y_ref):
    c = pl.program_id(1)
    n_chunks = pl.num_programs(1)

    @pl.when(c == 0)
    def _():
        ex_ref[0:8, :] = jnp.zeros((8, SSD_INNER), F32)
        eb_ref[0:8, :] = jnp.zeros((8, BC_W), F32)
        ec_ref[0:8, :] = jnp.zeros((8, BC_W), F32)
        st_t_ref[...] = jnp.zeros_like(st_t_ref)

    raw_x = xs_ref[...]
    raw_b = b_ref[...]
    raw_c = c_ref[...]

    @pl.when(c == n_chunks - 1)
    def _():
        tail_ref[:, 0:SSD_INNER] = raw_x[CHUNK - 3:CHUNK, :]
        tail_ref[:, SSD_INNER:SSD_INNER + BC_W] = raw_b[CHUNK - 3:CHUNK, :]
        tail_ref[:, SSD_INNER + BC_W:CONV_CH] = raw_c[CHUNK - 3:CHUNK, :]

    xs = _silu(_conv_chunk(ex_ref, raw_x, cwx_ref, cbx_ref))
    bm = _silu(_conv_chunk(eb_ref, raw_b, cwb_ref, cbb_ref))
    cm = _silu(_conv_chunk(ec_ref, raw_c, cwc_ref, cbc_ref))

    dt = _softplus(dt_ref[...] + dtb_ref[...])
    a = -jnp.exp(alog_ref[...])
    d_a = dt * a
    rows = lax.broadcasted_iota(jnp.int32, (CHUNK, CHUNK), 0)
    cols = lax.broadcasted_iota(jnp.int32, (CHUNK, CHUNK), 1)
    causal = rows >= cols
    a_cum = jnp.dot(causal.astype(F32), d_a, preferred_element_type=F32,
                    precision=lax.Precision.HIGHEST)
    a_cum_t = a_cum.T
    dt_t = dt.T
    a_last_col = a_cum_t[:, CHUNK - 1:CHUNK]
    exp_a_cum = jnp.exp(a_cum)
    w_t = dt_t * jnp.exp(a_last_col - a_cum_t)
    ea_col = jnp.exp(a_last_col)
    lo = cols < SSD_HEAD_DIM

    for g in range(SSD_GROUPS):
        bg = bm[:, g * SSD_STATE:(g + 1) * SSD_STATE]
        cg = cm[:, g * SSD_STATE:(g + 1) * SSD_STATE]
        cb = lax.dot_general(cg.astype(BF16), bg.astype(BF16), (((1,), (1,)), ((), ())),
                             preferred_element_type=F32)
        bg_t = bg.T
        for q in range(SSD_HPG // 2):
            e0 = g * SSD_HPG + 2 * q
            sl = slice((e0 // 2) * LANES, (e0 // 2 + 1) * LANES)
            lhs = []
            upd = []
            for e in (e0, e0 + 1):
                seg = a_cum[:, e:e + 1] - a_cum_t[e:e + 1, :]
                dec = jnp.where(causal, jnp.exp(jnp.where(causal, seg, 0.0)), 0.0)
                lhs.append((cb * dec * dt_t[e:e + 1, :]).astype(BF16))
                lhs.append((cg * exp_a_cum[:, e:e + 1]).astype(BF16))
                upd.append((bg_t * w_t[e:e + 1, :]).astype(BF16))
            xs_p = xs[:, sl]
            st_p = st_t_ref[:, sl]
            xs_lo = jnp.where(lo, xs_p, 0.0).astype(BF16)
            xs_hi = jnp.where(lo, 0.0, xs_p).astype(BF16)
            st_lo = jnp.where(lo, st_p, 0.0).astype(BF16)
            st_hi = jnp.where(lo, 0.0, st_p).astype(BF16)
            y_ref[:, sl] = jnp.dot(jnp.concatenate(lhs, axis=1),
                                   jnp.concatenate([xs_lo, st_lo, xs_hi, st_hi], axis=0),
                                   preferred_element_type=F32)
            dec_p = jnp.where(lo, ea_col[e0:e0 + 1, :], ea_col[e0 + 1:e0 + 2, :])
            st_t_ref[:, sl] = st_p * dec_p + jnp.dot(jnp.concatenate(upd, axis=1),
                                                     jnp.concatenate([xs_lo, xs_hi], axis=0),
                                                     preferred_element_type=F32)

    y = (y_ref[...] + dskip_ref[...] * xs) * z_ref[...]
    ya_ref[...] = _group_rmsnorm(y, nw_ref[...]).astype(BF16)

    @pl.when(c == n_chunks - 1)
    def _():
        st_ref[...] = st_t_ref[...].T

    v = (_ln(v_ref[...]) * vg_ref[...] + vb_ref[...]).astype(BF16)
    for g in range(SG_GROUPS):
        sl = slice(g * SG_GDIM, (g + 1) * SG_GDIM)
        w = jnp.where(causal, ws_ref[g], 0.0).astype(BF16)
        s = jnp.dot(w, v[:, sl], preferred_element_type=F32) + bst_ref[:, g:g + 1]
        yb_ref[:, sl] = (u_ref[:, sl] * s).astype(BF16)


def _mix_prompt(proj, lw, layer):
    n_chunks = SEQ // CHUNK

    def pspec(off, width):
        return pl.BlockSpec((CHUNK, width), lambda b, c: (b * n_chunks + c, off // width))

    def full(arr):
        return pl.BlockSpec(arr.shape, lambda b, c: (0,) * arr.ndim)

    params = [lw["cwx"], lw["cwb"], lw["cwc"], lw["cbx"], lw["cbb"], lw["cbc"],
              lw["dtb"], lw["alog"], lw["dskip"], lw["nw"],
              lw["vg"], lw["vb"], lw["ws"], lw["bst"]]
    return pl.pallas_call(
        _mix_prompt_kernel,
        grid=(BATCH, n_chunks),
        in_specs=[pspec(SEG_Z, SSD_INNER), pspec(SEG_XS, SSD_INNER), pspec(SEG_B, BC_W), pspec(SEG_C, BC_W),
                  pspec(SEG_DT, LANES), pspec(SEG_U, SG_WIDTH), pspec(SEG_V, SG_WIDTH)]
        + [full(p) for p in params],
        out_specs=[
            pl.BlockSpec((CHUNK, SSD_INNER), lambda b, c: (b * n_chunks + c, 0)),
            pl.BlockSpec((CHUNK, SG_WIDTH), lambda b, c: (b * n_chunks + c, 0)),
            pl.BlockSpec((None, SSD_INNER, SSD_STATE), lambda b, c: (b, 0, 0)),
            pl.BlockSpec((None, CONV_W - 1, CONV_CH), lambda b, c: (b, 0, 0)),
        ],
        out_shape=[
            jax.ShapeDtypeStruct((T_PROMPT, SSD_INNER), BF16),
            jax.ShapeDtypeStruct((T_PROMPT, SG_WIDTH), BF16),
            jax.ShapeDtypeStruct((BATCH, SSD_INNER, SSD_STATE), F32),
            jax.ShapeDtypeStruct((BATCH, CONV_W - 1, CONV_CH), F32),
        ],
        scratch_shapes=[
            pltpu.VMEM((CHUNK + 8, SSD_INNER), F32),
            pltpu.VMEM((CHUNK + 8, BC_W), F32),
            pltpu.VMEM((CHUNK + 8, BC_W), F32),
            pltpu.VMEM((SSD_STATE, SSD_INNER), F32),
            pltpu.VMEM((CHUNK, SSD_INNER), F32),
        ],
        compiler_params=_cparams(("arbitrary", "arbitrary")),
        name="mix_prompt",
    )(proj, proj, proj, proj, proj, proj, proj, *params)


def _sprep_kernel(xs_ref, b_ref, c_ref, dt_ref, u_ref, v_ref, cs_ref,
                  cwx_ref, cwb_ref, cwc_ref, cbx_ref, cbb_ref, cbc_ref,
                  dtb_ref, alog_ref, vg_ref, vb_ref, ws0_ref, bs0_ref,
                  xa_ref, bm_ref, cm_ref, dec_ref, dtx_ref, yb_ref, vo_ref, ncs_ref):
    raw = (xs_ref[...], b_ref[...], c_ref[...])
    offs = (0, SSD_INNER, SSD_INNER + BC_W)
    widths = (SSD_INNER, BC_W, BC_W)
    cws = (cwx_ref, cwb_ref, cwc_ref)
    cbs = (cbx_ref, cbb_ref, cbc_ref)
    act = []
    for r, off, wd, cw, cbias in zip(raw, offs, widths, cws, cbs):
        acc = cbias[...] + cw[CONV_W - 1:CONV_W, :] * r
        for k in range(CONV_W - 1):
            acc = acc + cw[k:k + 1, :] * cs_ref[:, k * CONV_CH + off:k * CONV_CH + off + wd]
        act.append(_silu(acc))
        ncs_ref[:, (CONV_W - 2) * CONV_CH + off:(CONV_W - 2) * CONV_CH + off + wd] = r
    ncs_ref[:, 0:(CONV_W - 2) * CONV_CH] = cs_ref[:, CONV_CH:(CONV_W - 1) * CONV_CH]
    xa, bm, cm = act
    xa_ref[...] = xa
    bm_ref[...] = bm
    cm_ref[...] = cm

    dt = _softplus(dt_ref[...] + dtb_ref[...])
    a = -jnp.exp(alog_ref[...])
    dec_t = jnp.exp(dt * a).T
    dt_t = dt.T
    for p in range(SSD_HEADS // 2):
        xa_t = xa[:, p * LANES:(p + 1) * LANES].T
        for half in range(2):
            h = 2 * p + half
            rs = slice(h * SSD_HEAD_DIM, (h + 1) * SSD_HEAD_DIM)
            dec_ref[rs, :] = jnp.broadcast_to(dec_t[h:h + 1, :], (SSD_HEAD_DIM, DEC_BATCH))
            dtx_ref[rs, :] = xa_t[half * SSD_HEAD_DIM:(half + 1) * SSD_HEAD_DIM, :] * dt_t[h:h + 1, :]

    v = _ln(v_ref[...]) * vg_ref[...] + vb_ref[...]
    vo_ref[...] = v
    yb_ref[...] = (u_ref[...] * (ws0_ref[...] * v + bs0_ref[...])).astype(BF16)


def _sprep(proj, conv_state, lw, layer):
    def pspec(off, width):
        return pl.BlockSpec((DEC_BATCH, width), lambda i: (0, off // width))

    def full(arr):
        return pl.BlockSpec(arr.shape, lambda i: (0,) * arr.ndim)

    params = [lw["cwx"], lw["cwb"], lw["cwc"], lw["cbx"], lw["cbb"], lw["cbc"],
              lw["dtb"], lw["alog"], lw["vg"], lw["vb"], lw["ws0"], lw["bs0"]]
    cw = (CONV_W - 1) * CONV_CH
    outs = [
        ((DEC_BATCH, SSD_INNER), F32), ((DEC_BATCH, BC_W), F32), ((DEC_BATCH, BC_W), F32),
        ((SSD_INNER, DEC_BATCH), F32), ((SSD_INNER, DEC_BATCH), F32),
        ((DEC_BATCH, SG_WIDTH), BF16), ((DEC_BATCH, SG_WIDTH), F32), ((DEC_BATCH, cw), F32),
    ]
    return pl.pallas_call(
        _sprep_kernel,
        grid=(1,),
        in_specs=[pspec(SEG_XS, SSD_INNER), pspec(SEG_B, BC_W), pspec(SEG_C, BC_W), pspec(SEG_DT, LANES),
                  pspec(SEG_U, SG_WIDTH), pspec(SEG_V, SG_WIDTH),
                  pl.BlockSpec((None, DEC_BATCH, cw), lambda i: (layer, 0, 0))]
        + [full(p) for p in params],
        out_specs=[pl.BlockSpec(s, lambda i: (0, 0)) for s, _ in outs],
        out_shape=[jax.ShapeDtypeStruct(s, d) for s, d in outs],
        compiler_params=_cparams(("arbitrary",)),
        name="sample_prep",
    )(proj, proj, proj, proj, proj, proj, conv_state, *params)


SSTATE_BT = 8


def _sstate_kernel(st_ref, dec_ref, dtx_ref, bm_ref, cm_ref, xa_ref, z_ref, dskip_ref, nw_ref,
                   *rest):
    sto_ref, ya_ref, yt_ref, ycol_ref = rest[-4:]
    i = pl.program_id(0)
    lane = lax.broadcasted_iota(jnp.int32, (SSD_INNER, DEC_BATCH), 1)

    @pl.when(i == 0)
    def _():
        yt_ref[...] = jnp.zeros_like(yt_ref)

    base = i * SSTATE_BT
    shift = (DEC_BATCH - base) % DEC_BATCH
    dec_r = pltpu.roll(dec_ref[...], shift, 1)
    dtx_r = pltpu.roll(dtx_ref[...], shift, 1)
    ycol_ref[...] = jnp.zeros_like(ycol_ref)
    gr = SSD_HPG * SSD_HEAD_DIM
    for j in range(SSTATE_BT):
        b_row = bm_ref[pl.ds(base + j, 1), :]
        c_row = cm_ref[pl.ds(base + j, 1), :]
        for g in range(SSD_GROUPS):
            rs = slice(g * gr, (g + 1) * gr)
            ns = slice(g * SSD_STATE, (g + 1) * SSD_STATE)
            hn = st_ref[j, rs, :] * dec_r[rs, j:j + 1] + dtx_r[rs, j:j + 1] * b_row[:, ns]
            sto_ref[j, rs, :] = hn
            ycol_ref[rs, j:j + 1] = jnp.sum(hn * c_row[:, ns], axis=1, keepdims=True)
    put = (lane >= base) & (lane < base + SSTATE_BT)
    yt_ref[...] = jnp.where(put, pltpu.roll(ycol_ref[...], base, 1), yt_ref[...])

    @pl.when(i == pl.num_programs(0) - 1)
    def _():
        xa = xa_ref[...]
        cols = []
        for k in range(SSD_INNER // LANES):
            cols.append(yt_ref[k * LANES:(k + 1) * LANES, :].T)
        y = jnp.concatenate(cols, axis=1)
        y = (y + dskip_ref[...] * xa) * z_ref[...]
        ya_ref[...] = _group_rmsnorm(y, nw_ref[...]).astype(BF16)


def _sstate(state_all, stacked_prev, prep, proj, lw, layer):
    xa, bm, cm, dec, dtx = prep

    def full(arr):
        return pl.BlockSpec(arr.shape, lambda i: (0,) * arr.ndim)

    in_specs = [
        pl.BlockSpec((None, SSTATE_BT, SSD_INNER, SSD_STATE), lambda i: (layer, i, 0, 0)),
        full(dec), full(dtx), full(bm), full(cm), full(xa),
        pl.BlockSpec((DEC_BATCH, SSD_INNER), lambda i: (0, SEG_Z // SSD_INNER)),
        full(lw["dskip"]), full(lw["nw"]),
    ]
    args = [state_all, dec, dtx, bm, cm, xa, proj, lw["dskip"], lw["nw"]]
    aliases = {}
    if stacked_prev is not None:
        in_specs.append(pl.BlockSpec(memory_space=pl.ANY))
        args.append(stacked_prev)
        aliases = {len(args) - 1: 0}
    return pl.pallas_call(
        _sstate_kernel,
        grid=(DEC_BATCH // SSTATE_BT,),
        in_specs=in_specs,
        out_specs=[
            pl.BlockSpec((None, SSTATE_BT, SSD_INNER, SSD_STATE), lambda i: (layer, i, 0, 0)),
            pl.BlockSpec((DEC_BATCH, SSD_INNER), lambda i: (0, 0)),
        ],
        out_shape=[
            jax.ShapeDtypeStruct((DEPTH, DEC_BATCH, SSD_INNER, SSD_STATE), F32),
            jax.ShapeDtypeStruct((DEC_BATCH, SSD_INNER), BF16),
        ],
        scratch_shapes=[pltpu.VMEM((SSD_INNER, DEC_BATCH), F32), pltpu.VMEM((SSD_INNER, DEC_BATCH), F32)],
        input_output_aliases=aliases,
        compiler_params=_cparams(("arbitrary",)),
        name="sample_state",
    )(*args)


def _merge_kernel(ya_ref, yb_ref, ga_ref, gb_ref, x_ref, gm_ref, scf_ref, shf_ref,
                  wpa_ref, wpb_ref, wo_ref, l1g_ref, l1b_ref, wr_ref, br_ref, *rest):
    x1_ref, h2_ref, rt_ref = rest[-3:]
    tm = x_ref.shape[0]
    rc = min(tm, MERGE_RC)
    for r in range(tm // rc):
        rows = slice(r * rc, (r + 1) * rc)
        mods = [ref[...] if ref.shape[0] == 1 else ref[rows, :] for ref in (gm_ref, scf_ref, shf_ref)]
        _merge_rows(rows, mods, ya_ref, yb_ref, ga_ref, gb_ref, x_ref, wpa_ref, wpb_ref, wo_ref,
                    l1g_ref, l1b_ref, wr_ref, br_ref, x1_ref, h2_ref, rt_ref)


def _merge_rows(rows, mods, ya_ref, yb_ref, ga_ref, gb_ref, x_ref, wpa_ref, wpb_ref, wo_ref,
                l1g_ref, l1b_ref, wr_ref, br_ref, x1_ref, h2_ref, rt_ref):
    g_m, sc_f, sh_f = mods
    t1 = jnp.dot(ya_ref[rows, :], wpa_ref[...], preferred_element_type=F32)
    t2 = jnp.dot(yb_ref[rows, :], wpb_ref[...], preferred_element_type=F32)
    m = (ga_ref[rows, :] * t1 + gb_ref[rows, :] * t2).astype(BF16)
    mix = jnp.dot(m, wo_ref[...], preferred_element_type=F32)
    x1 = _ln(DN_ALPHA * x_ref[rows, :] + g_m * mix) * l1g_ref[...] + l1b_ref[...]
    x1_ref[rows, :] = x1
    h2 = _ln(x1) * (1.0 + sc_f) + sh_f
    h2_ref[rows, :] = h2

    rl = jnp.dot(h2.astype(BF16), wr_ref[...], preferred_element_type=F32) + br_ref[...]
    lane = lax.broadcasted_iota(jnp.int32, rl.shape, 1)
    lane_f = lane.astype(F32)
    is_g = lane < N_EXP_GROUPS
    lg = jnp.where(is_g, rl, NEG_BIG)
    gmax = jnp.max(lg, axis=-1, keepdims=True)
    gidx = jnp.min(jnp.where(lg == gmax, lane_f, float(LANES)), axis=-1, keepdims=True)
    p_grp = 1.0 / jnp.sum(jnp.where(is_g, jnp.exp(lg - gmax), 0.0), axis=-1, keepdims=True)
    e_lane = lane - N_EXP_GROUPS
    in_grp = (e_lane >= 0) & (e_lane < N_EXPERTS) & (lax.shift_right_arithmetic(e_lane, 3).astype(F32) == gidx)
    le = jnp.where(in_grp, rl, NEG_BIG)
    m1 = jnp.max(le, axis=-1, keepdims=True)
    i1 = jnp.min(jnp.where(le == m1, lane_f, float(LANES)), axis=-1, keepdims=True)
    le2 = jnp.where(lane_f == i1, NEG_BIG, le)
    m2 = jnp.max(le2, axis=-1, keepdims=True)
    i2 = jnp.min(jnp.where(le2 == m2, lane_f, float(LANES)), axis=-1, keepdims=True)
    e2 = jnp.exp(m2 - m1)
    den = 1.0 + e2
    w1 = (1.0 / den) * p_grp
    w2 = (e2 / den) * p_grp
    rt_ref[rows, :] = jnp.where(lane == 0, i1 - N_EXP_GROUPS,
                            jnp.where(lane == 1, i2 - N_EXP_GROUPS,
                                      jnp.where(lane == 2, w1, jnp.where(lane == 3, w2, 0.0))))


MERGE_TM = 256
MERGE_RC = 256


def _merge(ya, yb, proj, x, mod, lw, layer, prompt, h2_prev):
    t = x.shape[0]
    tm = MERGE_TM if prompt else DEC_BATCH

    def const(shape):
        return pl.BlockSpec((None,) + shape, lambda i: (layer,) + (0,) * len(shape),
                            pipeline_mode=pl.Buffered(1))

    def gspec(off):
        return pl.BlockSpec((tm, D_MODEL), lambda i: (i, off // D_MODEL))

    in_specs = [
        pl.BlockSpec((tm, D_MODEL), lambda i: (i, 0)),
        pl.BlockSpec((tm, D_MODEL), lambda i: (i, 0)),
        gspec(SEG_GA), gspec(SEG_GB),
        pl.BlockSpec((tm, D_MODEL), lambda i: (i, 0)),
        _mod_spec(prompt, layer, MOD_G_M, tm),
        _mod_spec(prompt, layer, MOD_SC_F, tm),
        _mod_spec(prompt, layer, MOD_SH_F, tm),
        const((D_MODEL, D_MODEL)), const((D_MODEL, D_MODEL)), const((D_MODEL, D_MODEL)),
        const((1, D_MODEL)), const((1, D_MODEL)),
        const((D_MODEL, ROUTE_W)), const((1, ROUTE_W)),
    ]
    args = [ya, yb, proj, proj, x, mod, mod, mod, lw["w_pa"], lw["w_pb"], lw["w_o"],
            lw["ln1_g"], lw["ln1_b"], lw["w_r"], lw["b_r"]]
    aliases = {}
    h2_blk = 0
    if h2_prev is not None:
        in_specs.append(pl.BlockSpec(memory_space=pl.ANY))
        args.append(h2_prev)
        aliases = {len(args) - 1: 1}
        h2_blk = T_PROMPT // tm
    return pl.pallas_call(
        _merge_kernel,
        grid=(t // tm,),
        in_specs=in_specs,
        out_specs=[
            pl.BlockSpec((tm, D_MODEL), lambda i: (i, 0)),
            pl.BlockSpec((tm, D_MODEL), lambda i: (i + h2_blk, 0)),
            pl.BlockSpec((tm, ROUTE_W), lambda i: (i, 0)),
        ],
        out_shape=[
            jax.ShapeDtypeStruct((t, D_MODEL), F32),
            jax.ShapeDtypeStruct((T_ALL, D_MODEL), F32),
            jax.ShapeDtypeStruct((t, ROUTE_W), F32),
        ],
        input_output_aliases=aliases,
        compiler_params=_cparams(("arbitrary",)),
        name="merge_p" if prompt else "merge_s",
    )(*args)


MOE_ROWS = 128
MOE_MAX_PASSES = N_ASSIGN // MOE_ROWS + N_EXPERTS
MOE_ISSUE_UNROLL = 8


def _moe_kernel(pe_ref, first_ref, valid_ref, asg_ref, total_ref,
                h_hbm, wg_ref, wu_ref, wd_ref, out_hbm,
                xbuf0, xbuf1, ybuf0, ybuf1, wg_bf, wu_bf, wd_bf, sem_in, sem_out):
    p = pl.program_id(0)
    total = total_ref[0]

    def gather_row(q_base, r, xb, sem):
        tok = asg_ref[q_base + r] >> 1
        return pltpu.make_async_copy(h_hbm.at[pl.ds(tok, 1), :], xb.at[pl.ds(r, 1), :], sem)

    def wait_gather(xb, sem):
        pltpu.make_async_copy(h_hbm.at[pl.ds(0, MOE_ROWS), :], xb, sem).wait()

    def scatter_row(q_base, r, yb, sem):
        a = asg_ref[q_base + r]
        return pltpu.make_async_copy(yb.at[pl.ds(r, 1), :], out_hbm.at[a & 1, pl.ds(a >> 1, 1), :], sem)

    def issue_scatter(q_base, yb, sem, n):
        def body8(i, c):
            for u in range(MOE_ISSUE_UNROLL):
                scatter_row(q_base, i * MOE_ISSUE_UNROLL + u, yb, sem).start()
            return c

        def body1(r, c):
            scatter_row(q_base, r, yb, sem).start()
            return c

        n8 = n // MOE_ISSUE_UNROLL
        lax.fori_loop(0, n8, body8, 0)
        lax.fori_loop(n8 * MOE_ISSUE_UNROLL, n, body1, 0)

    def wait_scatter(yb, sem, n):
        bit = MOE_ROWS
        while bit >= 1:
            @pl.when((n & bit) != 0)
            def _(bit=bit):
                pltpu.make_async_copy(yb.at[pl.ds(0, bit), :], out_hbm.at[0, pl.ds(0, bit), :], sem).wait()
            bit //= 2

    def run_pass(x_cur, x_nxt, y_cur, y_oth, s_cur, s_nxt):
        @pl.when(first_ref[p] == 1)
        def _():
            wg_bf[...] = wg_ref[...].astype(BF16)
            wu_bf[...] = wu_ref[...].astype(BF16)
            wd_bf[...] = wd_ref[...].astype(BF16)

        wait_gather(x_cur, sem_in.at[s_cur])
        nxt_base = (p + 1) * MOE_ROWS
        for r in range(MOE_ROWS):
            gather_row(nxt_base, r, x_nxt, sem_in.at[s_nxt]).start()

        x = x_cur[...].astype(BF16)
        gate = jnp.dot(x, wg_bf[...], preferred_element_type=F32)
        up = jnp.dot(x, wu_bf[...], preferred_element_type=F32)
        hid = (_silu(gate) * up).astype(BF16)
        y = jnp.dot(hid, wd_bf[...], preferred_element_type=F32)

        @pl.when(p >= 2)
        def _():
            wait_scatter(y_cur, sem_out.at[s_cur], valid_ref[jnp.maximum(p - 2, 0)])

        y_cur[...] = y
        issue_scatter(p * MOE_ROWS, y_cur, sem_out.at[s_cur], valid_ref[p])

        @pl.when(p == total - 1)
        def _():
            wait_gather(x_nxt, sem_in.at[s_nxt])

            @pl.when(p >= 1)
            def _():
                wait_scatter(y_oth, sem_out.at[s_nxt], valid_ref[jnp.maximum(p - 1, 0)])

            wait_scatter(y_cur, sem_out.at[s_cur], valid_ref[p])

    @pl.when((p == 0) & (total > 0))
    def _():
        def body(r, c):
            gather_row(0, r, xbuf0, sem_in.at[0]).start()
            return c

        lax.fori_loop(0, MOE_ROWS, body, 0, unroll=MOE_ISSUE_UNROLL)

    @pl.when((p < total) & (p % 2 == 0))
    def _():
        run_pass(xbuf0, xbuf1, ybuf0, ybuf1, 0, 1)

    @pl.when((p < total) & (p % 2 == 1))
    def _():
        run_pass(xbuf1, xbuf0, ybuf1, ybuf0, 1, 0)


def _moe_passes(eid):
    i32 = jnp.int32
    order = jnp.argsort(eid).astype(i32)
    count = jnp.sum((eid[:, None] == jnp.arange(N_EXPERTS, dtype=i32)[None, :]).astype(i32), axis=0)
    start = jnp.cumsum(count) - count
    npass = (count + MOE_ROWS - 1) // MOE_ROWS
    pend = jnp.cumsum(npass)
    total = pend[-1]
    last_e = jnp.max(jnp.where(count > 0, jnp.arange(N_EXPERTS, dtype=i32), 0))
    pidx = jnp.arange(MOE_MAX_PASSES + 1, dtype=i32)
    pe = jnp.minimum(jnp.searchsorted(pend, pidx, side="right").astype(i32), last_e)
    k_in = pidx - (pend - npass)[pe]
    valid = jnp.clip(count[pe] - k_in * MOE_ROWS, 0, MOE_ROWS)
    first = ((k_in == 0) & (pidx < total)).astype(i32)
    row0 = start[pe] + k_in * MOE_ROWS
    r = jnp.arange(MOE_ROWS, dtype=i32)[None, :]
    rows = jnp.where(r < valid[:, None], row0[:, None] + r, row0[:, None])
    asg = order[jnp.clip(rows, 0, N_ASSIGN - 1)].reshape(-1)
    return pe, first, valid.astype(i32), asg, total.reshape(1).astype(i32)


def _moe(h2_all, eid, w_gate, w_up, w_down, layer):
    pe, first, valid, asg, total = _moe_passes(eid)

    def wspec(shape):
        return pl.BlockSpec((None, None) + shape, lambda p, pe, *_: (layer, pe[p], 0, 0))

    grid_spec = pltpu.PrefetchScalarGridSpec(
        num_scalar_prefetch=5,
        grid=(MOE_MAX_PASSES,),
        in_specs=[
            pl.BlockSpec(memory_space=pl.ANY),
            wspec((D_MODEL, D_EXPERT)), wspec((D_MODEL, D_EXPERT)), wspec((D_EXPERT, D_MODEL)),
        ],
        out_specs=pl.BlockSpec(memory_space=pl.ANY),
        scratch_shapes=[
            pltpu.VMEM((MOE_ROWS, D_MODEL), F32), pltpu.VMEM((MOE_ROWS, D_MODEL), F32),
            pltpu.VMEM((MOE_ROWS, D_MODEL), F32), pltpu.VMEM((MOE_ROWS, D_MODEL), F32),
            pltpu.VMEM((D_MODEL, D_EXPERT), BF16),
            pltpu.VMEM((D_MODEL, D_EXPERT), BF16),
            pltpu.VMEM((D_EXPERT, D_MODEL), BF16),
            pltpu.SemaphoreType.DMA((2,)),
            pltpu.SemaphoreType.DMA((2,)),
        ],
    )
    return pl.pallas_call(
        _moe_kernel,
        grid_spec=grid_spec,
        out_shape=jax.ShapeDtypeStruct((EXP_TOP_K, T_ALL, D_MODEL), F32),
        compiler_params=_cparams(("arbitrary",)),
        name="moe_experts",
    )(pe, first, valid, asg, total, h2_all, w_gate, w_up, w_down)


def _final_kernel(x1_ref, o0_ref, o1_ref, rt_ref, gf_ref, g_ref, b_ref, y_ref):
    rt = rt_ref[...]
    ffn = rt[:, 2:3] * o0_ref[...] + rt[:, 3:4] * o1_ref[...]
    y_ref[...] = _ln(DN_ALPHA * x1_ref[...] + gf_ref[...] * ffn) * g_ref[...] + b_ref[...]


FINAL_TM = 512


def _final(x1, out2, route, mod, lw, layer, prompt):
    t = x1.shape[0]
    tm = FINAL_TM if prompt else DEC_BATCH
    blk0 = 0 if prompt else T_PROMPT // tm

    def const(shape):
        return pl.BlockSpec((None,) + shape, lambda i: (layer,) + (0,) * len(shape))

    return pl.pallas_call(
        _final_kernel,
        grid=(t // tm,),
        in_specs=[
            pl.BlockSpec((tm, D_MODEL), lambda i: (i, 0)),
            pl.BlockSpec((None, tm, D_MODEL), lambda i: (0, i + blk0, 0)),
            pl.BlockSpec((None, tm, D_MODEL), lambda i: (1, i + blk0, 0)),
            pl.BlockSpec((tm, ROUTE_W), lambda i: (i, 0)),
            _mod_spec(prompt, layer, MOD_G_F, tm),
            const((1, D_MODEL)), const((1, D_MODEL)),
        ],
        out_specs=pl.BlockSpec((tm, D_MODEL), lambda i: (i, 0)),
        out_shape=jax.ShapeDtypeStruct((t, D_MODEL), F32),
        compiler_params=_cparams(("arbitrary",)),
        name="final_p" if prompt else "final_s",
    )(x1, out2, out2, route, mod, lw["ln2_g"], lw["ln2_b"])


def _layer_params(i, conv_w, conv_b, dt_bias, a_log, d_skip, ssd_norm_w, v_ln_g, v_ln_b, w_s, b_s,
                  stacked):
    pad_h = LANES - SSD_HEADS
    cw, cb = conv_w[i], conv_b[i][None, :]
    lw = dict(stacked)
    lw.update(
        cwx=cw[:, :SSD_INNER], cwb=cw[:, SSD_INNER:SSD_INNER + BC_W], cwc=cw[:, SSD_INNER + BC_W:],
        cbx=cb[:, :SSD_INNER], cbb=cb[:, SSD_INNER:SSD_INNER + BC_W], cbc=cb[:, SSD_INNER + BC_W:],
        dtb=jnp.pad(dt_bias[i], (0, pad_h))[None, :],
        alog=jnp.pad(a_log[i], (0, pad_h))[None, :],
        dskip=jnp.repeat(d_skip[i], SSD_HEAD_DIM)[None, :],
        nw=ssd_norm_w[i][None, :],
        vg=v_ln_g[i][None, :], vb=v_ln_b[i][None, :],
        ws=w_s[i], bst=b_s[i].T,
        ws0=jnp.repeat(w_s[i, :, 0, 0], SG_GDIM)[None, :],
        bs0=jnp.repeat(b_s[i, :, 0], SG_GDIM)[None, :],
    )
    return lw


def kernel(x_prompt, x_sample, state_ssd, state_conv, c_prompt, c_sample, w_ada, b_ada, w_in, conv_w, conv_b,
           dt_bias, a_log, d_skip, ssd_norm_w, v_ln_g, v_ln_b, w_s, b_s, w_pa, w_pb, w_o, ln1_g, ln1_b,
           w_rg, b_rg, w_re, b_re, w_gate, w_up, w_down, ln2_g, ln2_b):
    xp = x_prompt.reshape(T_PROMPT, D_MODEL)
    xs = x_sample.reshape(DEC_BATCH, D_MODEL)

    n_c = BATCH + DEC_BATCH
    c_all = jnp.pad(jnp.concatenate([c_sample, c_prompt], axis=0), ((0, (-n_c) % 8), (0, 0)))
    mod_s = _ada_mod(c_all, w_ada, b_ada)
    mod_p = mod_s[:, DEC_BATCH:n_c].reshape(DEPTH, BATCH, 1, 6 * D_MODEL)

    w_in_p = _pad_w_in(w_in)
    r_pad = ROUTE_W - N_EXP_GROUPS - N_EXPERTS
    stacked = dict(
        w_pa=w_pa.astype(BF16), w_pb=w_pb.astype(BF16), w_o=w_o.astype(BF16),
        ln1_g=ln1_g[:, None, :], ln1_b=ln1_b[:, None, :], ln2_g=ln2_g[:, None, :], ln2_b=ln2_b[:, None, :],
        w_r=jnp.pad(jnp.concatenate([w_rg, w_re], axis=-1), ((0, 0), (0, 0), (0, r_pad))).astype(BF16),
        b_r=jnp.pad(jnp.concatenate([b_rg, b_re], axis=-1), ((0, 0), (0, r_pad)))[:, None, :],
    )
    conv_state = state_conv.reshape(DEPTH, DEC_BATCH, (CONV_W - 1) * CONV_CH)
    state_all = state_ssd.reshape(DEPTH, DEC_BATCH, SSD_INNER, SSD_STATE)

    ssd_p, conv_p, conv_s, v_s = [], [], [], []
    ssd_s = None
    for i in range(DEPTH):
        lw = _layer_params(i, conv_w, conv_b, dt_bias, a_log, d_skip, ssd_norm_w, v_ln_g, v_ln_b, w_s, b_s,
                           stacked)
        proj_p = _inproj(xp, mod_p, w_in_p, i, True, 1024)
        proj_s = _inproj(xs, mod_s, w_in_p, i, False, DEC_BATCH)
        ya_p, yb_p, st_p, tail_p = _mix_prompt(proj_p, lw, i)
        xa, bm, cm, dec, dtx, yb_s, v_rows, ncs = _sprep(proj_s, conv_state, lw, i)
        ssd_s, ya_s = _sstate(state_all, ssd_s, (xa, bm, cm, dec, dtx), proj_s, lw, i)
        x1_p, h2_all, rt_p = _merge(ya_p, yb_p, proj_p, xp, mod_p, lw, i, True, None)
        x1_s, h2_all, rt_s = _merge(ya_s, yb_s, proj_s, xs, mod_s, lw, i, False, h2_all)
        route = jnp.concatenate([rt_p, rt_s], axis=0)
        eid = route[:, :EXP_TOP_K].astype(jnp.int32).reshape(-1)
        out2 = _moe(h2_all, eid, w_gate, w_up, w_down, i)
        xp = _final(x1_p, out2, rt_p, mod_p, lw, i, True)
        xs = _final(x1_s, out2, rt_s, mod_s, lw, i, False)

        ssd_p.append(st_p.reshape(BATCH, SSD_HEADS, SSD_HEAD_DIM, SSD_STATE))
        conv_p.append(tail_p)
        conv_s.append(ncs.reshape(DEC_BATCH, CONV_W - 1, CONV_CH))
        v_s.append(v_rows.reshape(DEC_BATCH, 1, SG_WIDTH))

    return (xp.reshape(BATCH, SEQ, D_MODEL), xs.reshape(DEC_BATCH, 1, D_MODEL),
            jnp.stack(ssd_p), jnp.stack(conv_p),
            ssd_s.reshape(DEPTH, DEC_BATCH, SSD_HEADS, SSD_HEAD_DIM, SSD_STATE),
            jnp.stack(conv_s), jnp.stack(v_s))
```

```python
import functools
import math

import numpy as np
import jax
import jax.numpy as jnp
from jax import lax
from jax.experimental import pallas as pl
from jax.experimental.pallas import tpu as pltpu

F32 = jnp.float32
BF16 = jnp.bfloat16

D_MODEL = 2048
BATCH = 4
SEQ = 2048
DEPTH = 4
DEC_BATCH = 128
SSD_HEAD_DIM = 64
SSD_INNER = D_MODEL
SSD_HEADS = SSD_INNER // SSD_HEAD_DIM
SSD_GROUPS = 4
SSD_HPG = SSD_HEADS // SSD_GROUPS
SSD_STATE = 128
CONV_W = 4
BC_W = SSD_GROUPS * SSD_STATE
CONV_CH = SSD_INNER + 2 * BC_W
SG_CHUNK = 128
SG_WIDTH = D_MODEL
SG_GROUPS = 8
SG_GDIM = SG_WIDTH // SG_GROUPS
N_EXP_GROUPS = 8
EXP_PER_GROUP = 8
N_EXPERTS = 64
EXP_TOP_K = 2
D_EXPERT = D_MODEL // 4
DN_ALPHA = (2.0 * DEPTH) ** 0.25
NORM_EPS = 1e-5

T_PROMPT = BATCH * SEQ
T_ALL = T_PROMPT + DEC_BATCH
N_ASSIGN = T_ALL * EXP_TOP_K

LANES = 128
CHUNK = 128
VMEM_LIMIT = 56 * 1024 * 1024

SEG_Z = 0
SEG_XS = 2048
SEG_U = 4096
SEG_V = 6144
SEG_GA = 8192
SEG_GB = 10240
SEG_B = 12288
SEG_C = 12800
SEG_DT = 13312
PROJ_W = 13824
PROJ_TN = 512
ROUTE_W = LANES

SQRT_HALF = np.float32(np.sqrt(0.5))
NEG_BIG = -3.0e38


def _cparams(sem):
    return pltpu.CompilerParams(dimension_semantics=sem, vmem_limit_bytes=VMEM_LIMIT)


def _silu(x):
    return x * jax.nn.sigmoid(x)


def _gelu(x):
    return 0.5 * x * (1.0 + lax.erf(x * SQRT_HALF))


def _softplus(x):
    return jnp.maximum(x, 0.0) + jnp.log1p(jnp.exp(-jnp.abs(x)))


def _ln(x):
    mu = jnp.mean(x, axis=-1, keepdims=True)
    xc = x - mu
    var = jnp.mean(xc * xc, axis=-1, keepdims=True)
    return xc * lax.rsqrt(var + NORM_EPS)


ADA_TN = 1024


def _ada_kernel(c_ref, w_ref, b_ref, o_ref):
    a = _silu(c_ref[...]).astype(BF16)
    o_ref[...] = jnp.dot(a, w_ref[...].astype(BF16), preferred_element_type=F32) + b_ref[...]


def _ada_mod(c_all, w_ada, b_ada):
    rows = c_all.shape[0]
    n = 6 * D_MODEL
    return pl.pallas_call(
        _ada_kernel,
        grid=(DEPTH, n // ADA_TN),
        in_specs=[
            pl.BlockSpec((rows, D_MODEL), lambda l, j: (0, 0)),
            pl.BlockSpec((None, D_MODEL, ADA_TN), lambda l, j: (l, 0, j)),
            pl.BlockSpec((None, 1, ADA_TN), lambda l, j: (l, 0, j)),
        ],
        out_specs=pl.BlockSpec((None, rows, ADA_TN), lambda l, j: (l, 0, j)),
        out_shape=jax.ShapeDtypeStruct((DEPTH, rows, n), F32),
        compiler_params=_cparams(("arbitrary", "arbitrary")),
        name="ada_mod",
    )(c_all, w_ada, b_ada.reshape(DEPTH, 1, n))


MOD_SH_M, MOD_SC_M, MOD_G_M, MOD_SH_F, MOD_SC_F, MOD_G_F = range(6)


def _mod_spec(prompt, layer, k, tm):
    if prompt:
        return pl.BlockSpec((None, None, 1, D_MODEL), lambda i, *_: (layer, (i * tm) // SEQ, 0, k))
    return pl.BlockSpec((None, tm, D_MODEL), lambda i, *_: (layer, 0, k))


INPROJ_RC = 512
IN_COLS = SSD_INNER + CONV_CH + SSD_HEADS + 2 * SG_WIDTH + 2 * D_MODEL
SRC_DT = SSD_INNER + CONV_CH


def _inproj_kernel(x_ref, sc_ref, sh_ref, w_ref, o_ref, h_ref, w_bf):
    j = pl.program_id(1)

    @pl.when(j == 0)
    def _():
        h = _ln(x_ref[...]) * (1.0 + sc_ref[...]) + sh_ref[...]
        h_ref[...] = h.astype(BF16)

    col = j * PROJ_TN
    tm = h_ref.shape[0]
    rc = min(tm, INPROJ_RC)

    def emit(act):
        w_bf[...] = w_ref[0].astype(BF16)
        for r in range(tm // rc):
            rows = slice(r * rc, (r + 1) * rc)
            o_ref[rows, :] = act(lax.dot_general(h_ref[rows, :], w_bf[...], (((1,), (1,)), ((), ())),
                                                 preferred_element_type=F32))

    @pl.when(col < SEG_XS)
    def _():
        emit(_silu)

    @pl.when((col >= SEG_U) & (col < SEG_GA))
    def _():
        emit(_gelu)

    @pl.when((col >= SEG_GA) & (col < SEG_B))
    def _():
        emit(jax.nn.sigmoid)

    @pl.when(((col >= SEG_XS) & (col < SEG_U)) | (col >= SEG_B))
    def _():
        emit(lambda a: a)


def _w_in_row(j):
    t_u, t_b, t_dt = SEG_U // PROJ_TN, SEG_B // PROJ_TN, SEG_DT // PROJ_TN
    return jnp.where(j < t_u, j * PROJ_TN,
                     jnp.where(j < t_b, SRC_DT + SSD_HEADS + (j - t_u) * PROJ_TN,
                               jnp.where(j < t_dt, 2 * SSD_INNER + (j - t_b) * PROJ_TN, SRC_DT)))


def _inproj(x, mod, w_in_t, layer, prompt, tm):
    t = x.shape[0]
    return pl.pallas_call(
        _inproj_kernel,
        grid=(t // tm, PROJ_W // PROJ_TN),
        in_specs=[
            pl.BlockSpec((tm, D_MODEL), lambda i, j: (i, 0)),
            _mod_spec(prompt, layer, MOD_SC_M, tm),
            _mod_spec(prompt, layer, MOD_SH_M, tm),
            pl.BlockSpec((pl.Element(1), pl.Element(PROJ_TN), pl.Element(D_MODEL)),
                         lambda i, j: (layer, pl.multiple_of(_w_in_row(j), 8), 0)),
        ],
        out_specs=pl.BlockSpec((tm, PROJ_TN), lambda i, j: (i, j)),
        out_shape=jax.ShapeDtypeStruct((t, PROJ_W), F32),
        scratch_shapes=[pltpu.VMEM((tm, D_MODEL), BF16), pltpu.VMEM((PROJ_TN, D_MODEL), BF16)],
        compiler_params=_cparams(("arbitrary", "arbitrary")),
        name="inproj_p" if prompt else "inproj_s",
    )(x, mod, mod, w_in_t)


def _conv_chunk(ext_ref, raw, w_ref, b_ref):
    ext_ref[8:8 + CHUNK, :] = raw
    acc = b_ref[...] + w_ref[CONV_W - 1:CONV_W, :] * raw
    for k in range(CONV_W - 1):
        acc = acc + w_ref[k:k + 1, :] * ext_ref[5 + k:5 + k + CHUNK, :]
    ext_ref[5:8, :] = ext_ref[5 + CHUNK:8 + CHUNK, :]
    return acc


def _group_rmsnorm(y, norm_w):
    gw = SSD_INNER // SSD_GROUPS
    outs = []
    for g in range(SSD_GROUPS):
        yg = y[:, g * gw:(g + 1) * gw]
        ms = jnp.mean(yg * yg, axis=-1, keepdims=True)
        outs.append(yg * lax.rsqrt(ms + NORM_EPS))
    return jnp.concatenate(outs, axis=-1) * norm_w


def _mix_prompt_kernel(z_ref, xs_ref, b_ref, c_ref, dt_ref, u_ref, v_ref,
                       cwx_ref, cwb_ref, cwc_ref, cbx_ref, cbb_ref, cbc_ref,
                       dtb_ref, alog_ref, dskip_ref, nw_ref,
                       vg_ref, vb_ref, ws_ref, bst_ref,
                       ya_ref, yb_ref, st_ref, tail_ref,
                       ex_ref, eb_ref, ec_ref, st_t_ref, y_ref):
    c = pl.program_id(1)
    n_chunks = pl.num_programs(1)

    @pl.when(c == 0)
    def _():
        ex_ref[0:8, :] = jnp.zeros((8, SSD_INNER), F32)
        eb_ref[0:8, :] = jnp.zeros((8, BC_W), F32)
        ec_ref[0:8, :] = jnp.zeros((8, BC_W), F32)
        st_t_ref[...] = jnp.zeros_like(st_t_ref)

    raw_x = xs_ref[...]
    raw_b = b_ref[...]
    raw_c = c_ref[...]

    @pl.when(c == n_chunks - 1)
    def _():
        tail_ref[:, 0:SSD_INNER] = raw_x[CHUNK - 3:CHUNK, :]
        tail_ref[:, SSD_INNER:SSD_INNER + BC_W] = raw_b[CHUNK - 3:CHUNK, :]
        tail_ref[:, SSD_INNER + BC_W:CONV_CH] = raw_c[CHUNK - 3:CHUNK, :]

    xs = _silu(_conv_chunk(ex_ref, raw_x, cwx_ref, cbx_ref))
    bm = _silu(_conv_chunk(eb_ref, raw_b, cwb_ref, cbb_ref))
    cm = _silu(_conv_chunk(ec_ref, raw_c, cwc_ref, cbc_ref))

    dt = _softplus(dt_ref[...] + dtb_ref[...])
    a = -jnp.exp(alog_ref[...])
    d_a = dt * a
    rows = lax.broadcasted_iota(jnp.int32, (CHUNK, CHUNK), 0)
    cols = lax.broadcasted_iota(jnp.int32, (CHUNK, CHUNK), 1)
    causal = rows >= cols
    a_cum = jnp.dot(causal.astype(F32), d_a, preferred_element_type=F32,
                    precision=lax.Precision.HIGHEST)
    a_cum_t = a_cum.T
    dt_t = dt.T
    a_last_col = a_cum_t[:, CHUNK - 1:CHUNK]
    exp_a_cum = jnp.exp(a_cum)
    w_t = dt_t * jnp.exp(a_last_col - a_cum_t)
    ea_col = jnp.exp(a_last_col)
    lo = cols < SSD_HEAD_DIM

    for g in range(SSD_GROUPS):
        bg = bm[:, g * SSD_STATE:(g + 1) * SSD_STATE]
        cg = cm[:, g * SSD_STATE:(g + 1) * SSD_STATE]
        cb = lax.dot_general(cg.astype(BF16), bg.astype(BF16), (((1,), (1,)), ((), ())),
                             preferred_element_type=F32)
        bg_t = bg.T
        for q in range(SSD_HPG // 2):
            e0 = g * SSD_HPG + 2 * q
            sl = slice((e0 // 2) * LANES, (e0 // 2 + 1) * LANES)
            lhs = []
            upd = []
            for e in (e0, e0 + 1):
                seg = a_cum[:, e:e + 1] - a_cum_t[e:e + 1, :]
                dec = jnp.where(causal, jnp.exp(jnp.where(causal, seg, 0.0)), 0.0)
                lhs.append((cb * dec * dt_t[e:e + 1, :]).astype(BF16))
                lhs.append((cg * exp_a_cum[:, e:e + 1]).astype(BF16))
                upd.append((bg_t * w_t[e:e + 1, :]).astype(BF16))
            xs_p = xs[:, sl]
            st_p = st_t_ref[:, sl]
            xs_lo = jnp.where(lo, xs_p, 0.0).astype(BF16)
            xs_hi = jnp.where(lo, 0.0, xs_p).astype(BF16)
            st_lo = jnp.where(lo, st_p, 0.0).astype(BF16)
            st_hi = jnp.where(lo, 0.0, st_p).astype(BF16)
            y_ref[:, sl] = jnp.dot(jnp.concatenate(lhs, axis=1),
                                   jnp.concatenate([xs_lo, st_lo, xs_hi, st_hi], axis=0),
                                   preferred_element_type=F32)
            dec_p = jnp.where(lo, ea_col[e0:e0 + 1, :], ea_col[e0 + 1:e0 + 2, :])
            st_t_ref[:, sl] = st_p * dec_p + jnp.dot(jnp.concatenate(upd, axis=1),
                                                     jnp.concatenate([xs_lo, xs_hi], axis=0),
                                                     preferred_element_type=F32)

    y = (y_ref[...] + dskip_ref[...] * xs) * z_ref[...]
    ya_ref[...] = _group_rmsnorm(y, nw_ref[...]).astype(BF16)

    @pl.when(c == n_chunks - 1)
    def _():
        st_ref[...] = st_t_ref[...].T

    v = (_ln(v_ref[...]) * vg_ref[...] + vb_ref[...]).astype(BF16)
    for g in range(SG_GROUPS):
        sl = slice(g * SG_GDIM, (g + 1) * SG_GDIM)
        w = jnp.where(causal, ws_ref[g], 0.0).astype(BF16)
        s = jnp.dot(w, v[:, sl], preferred_element_type=F32) + bst_ref[:, g:g + 1]
        yb_ref[:, sl] = (u_ref[:, sl] * s).astype(BF16)


def _mix_prompt(proj, lw, layer):
    n_chunks = SEQ // CHUNK

    def pspec(off, width):
        return pl.BlockSpec((CHUNK, width), lambda b, c: (b * n_chunks + c, off // width))

    def full(arr):
        return pl.BlockSpec(arr.shape, lambda b, c: (0,) * arr.ndim)

    params = [lw["cwx"], lw["cwb"], lw["cwc"], lw["cbx"], lw["cbb"], lw["cbc"],
              lw["dtb"], lw["alog"], lw["dskip"], lw["nw"],
              lw["vg"], lw["vb"], lw["ws"], lw["bst"]]
    return pl.pallas_call(
        _mix_prompt_kernel,
        grid=(BATCH, n_chunks),
        in_specs=[pspec(SEG_Z, SSD_INNER), pspec(SEG_XS, SSD_INNER), pspec(SEG_B, BC_W), pspec(SEG_C, BC_W),
                  pspec(SEG_DT, LANES), pspec(SEG_U, SG_WIDTH), pspec(SEG_V, SG_WIDTH)]
        + [full(p) for p in params],
        out_specs=[
            pl.BlockSpec((CHUNK, SSD_INNER), lambda b, c: (b * n_chunks + c, 0)),
            pl.BlockSpec((CHUNK, SG_WIDTH), lambda b, c: (b * n_chunks + c, 0)),
            pl.BlockSpec((None, SSD_INNER, SSD_STATE), lambda b, c: (b, 0, 0)),
            pl.BlockSpec((None, CONV_W - 1, CONV_CH), lambda b, c: (b, 0, 0)),
        ],
        out_shape=[
            jax.ShapeDtypeStruct((T_PROMPT, SSD_INNER), BF16),
            jax.ShapeDtypeStruct((T_PROMPT, SG_WIDTH), BF16),
            jax.ShapeDtypeStruct((BATCH, SSD_INNER, SSD_STATE), F32),
            jax.ShapeDtypeStruct((BATCH, CONV_W - 1, CONV_CH), F32),
        ],
        scratch_shapes=[
            pltpu.VMEM((CHUNK + 8, SSD_INNER), F32),
            pltpu.VMEM((CHUNK + 8, BC_W), F32),
            pltpu.VMEM((CHUNK + 8, BC_W), F32),
            pltpu.VMEM((SSD_STATE, SSD_INNER), F32),
            pltpu.VMEM((CHUNK, SSD_INNER), F32),
        ],
        compiler_params=_cparams(("arbitrary", "arbitrary")),
        name="mix_prompt",
    )(proj, proj, proj, proj, proj, proj, proj, *params)


def _sprep_kernel(xs_ref, b_ref, c_ref, dt_ref, u_ref, v_ref, cs_ref,
                  cwx_ref, cwb_ref, cwc_ref, cbx_ref, cbb_ref, cbc_ref,
                  dtb_ref, alog_ref, vg_ref, vb_ref, ws0_ref, bs0_ref,
                  xa_ref, bm_ref, cm_ref, dec_ref, dtx_ref, yb_ref, vo_ref, ncs_ref):
    raw = (xs_ref[...], b_ref[...], c_ref[...])
    offs = (0, SSD_INNER, SSD_INNER + BC_W)
    widths = (SSD_INNER, BC_W, BC_W)
    cws = (cwx_ref, cwb_ref, cwc_ref)
    cbs = (cbx_ref, cbb_ref, cbc_ref)
    act = []
    for r, off, wd, cw, cbias in zip(raw, offs, widths, cws, cbs):
        acc = cbias[...] + cw[CONV_W - 1:CONV_W, :] * r
        for k in range(CONV_W - 1):
            acc = acc + cw[k:k + 1, :] * cs_ref[:, k * CONV_CH + off:k * CONV_CH + off + wd]
        act.append(_silu(acc))
        ncs_ref[:, (CONV_W - 2) * CONV_CH + off:(CONV_W - 2) * CONV_CH + off + wd] = r
    ncs_ref[:, 0:(CONV_W - 2) * CONV_CH] = cs_ref[:, CONV_CH:(CONV_W - 1) * CONV_CH]
    xa, bm, cm = act
    xa_ref[...] = xa
    bm_ref[...] = bm
    cm_ref[...] = cm

    dt = _softplus(dt_ref[...] + dtb_ref[...])
    a = -jnp.exp(alog_ref[...])
    dec_ref[...] = jnp.exp(dt * a).T
    dt_t = dt.T
    for p in range(SSD_HEADS // 2):
        xa_t = xa[:, p * LANES:(p + 1) * LANES].T
        for half in range(2):
            h = 2 * p + half
            rs = slice(h * SSD_HEAD_DIM, (h + 1) * SSD_HEAD_DIM)
            dtx_ref[rs, :] = xa_t[half * SSD_HEAD_DIM:(half + 1) * SSD_HEAD_DIM, :] * dt_t[h:h + 1, :]

    v = _ln(v_ref[...]) * vg_ref[...] + vb_ref[...]
    vo_ref[...] = v
    yb_ref[...] = (u_ref[...] * (ws0_ref[...] * v + bs0_ref[...])).astype(BF16)


def _sprep(proj, conv_state, lw, layer):
    def pspec(off, width):
        return pl.BlockSpec((DEC_BATCH, width), lambda i: (0, off // width))

    def full(arr):
        return pl.BlockSpec(arr.shape, lambda i: (0,) * arr.ndim)

    params = [lw["cwx"], lw["cwb"], lw["cwc"], lw["cbx"], lw["cbb"], lw["cbc"],
              lw["dtb"], lw["alog"], lw["vg"], lw["vb"], lw["ws0"], lw["bs0"]]
    cw = (CONV_W - 1) * CONV_CH
    outs = [
        ((DEC_BATCH, SSD_INNER), F32), ((DEC_BATCH, BC_W), F32), ((DEC_BATCH, BC_W), F32),
        ((LANES, DEC_BATCH), F32), ((SSD_INNER, DEC_BATCH), F32),
        ((DEC_BATCH, SG_WIDTH), BF16), ((DEC_BATCH, SG_WIDTH), F32), ((DEC_BATCH, cw), F32),
    ]
    return pl.pallas_call(
        _sprep_kernel,
        grid=(1,),
        in_specs=[pspec(SEG_XS, SSD_INNER), pspec(SEG_B, BC_W), pspec(SEG_C, BC_W), pspec(SEG_DT, LANES),
                  pspec(SEG_U, SG_WIDTH), pspec(SEG_V, SG_WIDTH),
                  pl.BlockSpec((None, DEC_BATCH, cw), lambda i: (layer, 0, 0))]
        + [full(p) for p in params],
        out_specs=[pl.BlockSpec(s, lambda i: (0, 0)) for s, _ in outs],
        out_shape=[jax.ShapeDtypeStruct(s, d) for s, d in outs],
        compiler_params=_cparams(("arbitrary",)),
        name="sample_prep",
    )(proj, proj, proj, proj, proj, proj, conv_state, *params)


SSTATE_BT = 8


def _sstate_kernel(st_ref, dec_ref, dtx_ref, bm_ref, cm_ref, xa_ref, z_ref, dskip_ref, nw_ref,
                   *rest):
    sto_ref, ya_ref, yt_ref, ycol_ref = rest[-4:]
    i = pl.program_id(0)
    lane = lax.broadcasted_iota(jnp.int32, (SSD_INNER, DEC_BATCH), 1)

    @pl.when(i == 0)
    def _():
        yt_ref[...] = jnp.zeros_like(yt_ref)

    base = i * SSTATE_BT
    shift = (DEC_BATCH - base) % DEC_BATCH
    dec_r = pltpu.roll(dec_ref[...], shift, 1)
    dtx_r = pltpu.roll(dtx_ref[...], shift, 1)
    ycol_ref[...] = jnp.zeros_like(ycol_ref)
    gr = SSD_HPG * SSD_HEAD_DIM
    for j in range(SSTATE_BT):
        b_row = bm_ref[pl.ds(base + j, 1), :]
        c_row = cm_ref[pl.ds(base + j, 1), :]
        for g in range(SSD_GROUPS):
            rs = slice(g * gr, (g + 1) * gr)
            ns = slice(g * SSD_STATE, (g + 1) * SSD_STATE)
            dec_g = jnp.concatenate(
                [jnp.broadcast_to(dec_r[h:h + 1, j:j + 1], (SSD_HEAD_DIM, SSD_STATE))
                 for h in range(g * SSD_HPG, (g + 1) * SSD_HPG)], axis=0)
            hn = st_ref[j, rs, :] * dec_g + dtx_r[rs, j:j + 1] * b_row[:, ns]
            sto_ref[j, rs, :] = hn
            ycol_ref[rs, j:j + 1] = jnp.sum(hn * c_row[:, ns], axis=1, keepdims=True)
    put = (lane >= base) & (lane < base + SSTATE_BT)
    yt_ref[...] = jnp.where(put, pltpu.roll(ycol_ref[...], base, 1), yt_ref[...])

    @pl.when(i == pl.num_programs(0) - 1)
    def _():
        xa = xa_ref[...]
        cols = []
        for k in range(SSD_INNER // LANES):
            cols.append(yt_ref[k * LANES:(k + 1) * LANES, :].T)
        y = jnp.concatenate(cols, axis=1)
        y = (y + dskip_ref[...] * xa) * z_ref[...]
        ya_ref[...] = _group_rmsnorm(y, nw_ref[...]).astype(BF16)


def _sstate(state_all, stacked_prev, prep, proj, lw, layer):
    xa, bm, cm, dec, dtx = prep

    def full(arr):
        return pl.BlockSpec(arr.shape, lambda i: (0,) * arr.ndim)

    in_specs = [
        pl.BlockSpec((None, SSTATE_BT, SSD_INNER, SSD_STATE), lambda i: (layer, i, 0, 0)),
        full(dec), full(dtx), full(bm), full(cm), full(xa),
        pl.BlockSpec((DEC_BATCH, SSD_INNER), lambda i: (0, SEG_Z // SSD_INNER)),
        full(lw["dskip"]), full(lw["nw"]),
    ]
    args = [state_all, dec, dtx, bm, cm, xa, proj, lw["dskip"], lw["nw"]]
    aliases = {}
    if stacked_prev is not None:
        in_specs.append(pl.BlockSpec(memory_space=pl.ANY))
        args.append(stacked_prev)
        aliases = {len(args) - 1: 0}
    return pl.pallas_call(
        _sstate_kernel,
        grid=(DEC_BATCH // SSTATE_BT,),
        in_specs=in_specs,
        out_specs=[
            pl.BlockSpec((None, SSTATE_BT, SSD_INNER, SSD_STATE), lambda i: (layer, i, 0, 0)),
            pl.BlockSpec((DEC_BATCH, SSD_INNER), lambda i: (0, 0)),
        ],
        out_shape=[
            jax.ShapeDtypeStruct((DEPTH, DEC_BATCH, SSD_INNER, SSD_STATE), F32),
            jax.ShapeDtypeStruct((DEC_BATCH, SSD_INNER), BF16),
        ],
        scratch_shapes=[pltpu.VMEM((SSD_INNER, DEC_BATCH), F32), pltpu.VMEM((SSD_INNER, DEC_BATCH), F32)],
        input_output_aliases=aliases,
        compiler_params=_cparams(("arbitrary",)),
        name="sample_state",
    )(*args)


def _merge_kernel(ya_ref, yb_ref, ga_ref, gb_ref, x_ref, gm_ref, scf_ref, shf_ref,
                  wpa_ref, wpb_ref, wo_ref, l1g_ref, l1b_ref, wr_ref, br_ref, *rest):
    x1_ref, h2_ref, rt_ref = rest[-3:]
    g_m, sc_f, sh_f = gm_ref[...], scf_ref[...], shf_ref[...]
    t1 = jnp.dot(ya_ref[...], wpa_ref[...], preferred_element_type=F32)
    t2 = jnp.dot(yb_ref[...], wpb_ref[...], preferred_element_type=F32)
    m = (ga_ref[...] * t1 + gb_ref[...] * t2).astype(BF16)
    mix = jnp.dot(m, wo_ref[...], preferred_element_type=F32)
    x1 = _ln(DN_ALPHA * x_ref[...] + g_m * mix) * l1g_ref[...] + l1b_ref[...]
    x1_ref[...] = x1
    h2 = _ln(x1) * (1.0 + sc_f) + sh_f
    h2_ref[...] = h2

    rl = jnp.dot(h2.astype(BF16), wr_ref[...], preferred_element_type=F32) + br_ref[...]
    lane = lax.broadcasted_iota(jnp.int32, rl.shape, 1)
    lane_f = lane.astype(F32)
    is_g = lane < N_EXP_GROUPS
    lg = jnp.where(is_g, rl, NEG_BIG)
    gmax = jnp.max(lg, axis=-1, keepdims=True)
    gidx = jnp.min(jnp.where(lg == gmax, lane_f, float(LANES)), axis=-1, keepdims=True)
    p_grp = 1.0 / jnp.sum(jnp.where(is_g, jnp.exp(lg - gmax), 0.0), axis=-1, keepdims=True)
    e_lane = lane - N_EXP_GROUPS
    in_grp = (e_lane >= 0) & (e_lane < N_EXPERTS) & (lax.shift_right_arithmetic(e_lane, 3).astype(F32) == gidx)
    le = jnp.where(in_grp, rl, NEG_BIG)
    m1 = jnp.max(le, axis=-1, keepdims=True)
    i1 = jnp.min(jnp.where(le == m1, lane_f, float(LANES)), axis=-1, keepdims=True)
    le2 = jnp.where(lane_f == i1, NEG_BIG, le)
    m2 = jnp.max(le2, axis=-1, keepdims=True)
    i2 = jnp.min(jnp.where(le2 == m2, lane_f, float(LANES)), axis=-1, keepdims=True)
    e2 = jnp.exp(m2 - m1)
    den = 1.0 + e2
    w1 = (1.0 / den) * p_grp
    w2 = (e2 / den) * p_grp
    rt_ref[...] = jnp.where(lane == 0, i1 - N_EXP_GROUPS,
                            jnp.where(lane == 1, i2 - N_EXP_GROUPS,
                                      jnp.where(lane == 2, w1, jnp.where(lane == 3, w2, 0.0))))


MERGE_TM = 256


def _merge(ya, yb, proj, x, mod, lw, layer, prompt, h2_prev):
    t = x.shape[0]
    tm = MERGE_TM if prompt else DEC_BATCH

    def const(shape):
        return pl.BlockSpec((None,) + shape, lambda i: (layer,) + (0,) * len(shape),
                            pipeline_mode=pl.Buffered(1))

    def gspec(off):
        return pl.BlockSpec((tm, D_MODEL), lambda i: (i, off // D_MODEL))

    in_specs = [
        pl.BlockSpec((tm, D_MODEL), lambda i: (i, 0)),
        pl.BlockSpec((tm, D_MODEL), lambda i: (i, 0)),
        gspec(SEG_GA), gspec(SEG_GB),
        pl.BlockSpec((tm, D_MODEL), lambda i: (i, 0)),
        _mod_spec(prompt, layer, MOD_G_M, tm),
        _mod_spec(prompt, layer, MOD_SC_F, tm),
        _mod_spec(prompt, layer, MOD_SH_F, tm),
        const((D_MODEL, D_MODEL)), const((D_MODEL, D_MODEL)), const((D_MODEL, D_MODEL)),
        const((1, D_MODEL)), const((1, D_MODEL)),
        const((D_MODEL, ROUTE_W)), const((1, ROUTE_W)),
    ]
    args = [ya, yb, proj, proj, x, mod, mod, mod, lw["w_pa"], lw["w_pb"], lw["w_o"],
            lw["ln1_g"], lw["ln1_b"], lw["w_r"], lw["b_r"]]
    aliases = {}
    h2_blk = 0
    if h2_prev is not None:
        in_specs.append(pl.BlockSpec(memory_space=pl.ANY))
        args.append(h2_prev)
        aliases = {len(args) - 1: 1}
        h2_blk = T_PROMPT // tm
    return pl.pallas_call(
        _merge_kernel,
        grid=(t // tm,),
        in_specs=in_specs,
        out_specs=[
            pl.BlockSpec((tm, D_MODEL), lambda i: (i, 0)),
            pl.BlockSpec((tm, D_MODEL), lambda i: (i + h2_blk, 0)),
            pl.BlockSpec((tm, ROUTE_W), lambda i: (i, 0)),
        ],
        out_shape=[
            jax.ShapeDtypeStruct((t, D_MODEL), F32),
            jax.ShapeDtypeStruct((T_ALL, D_MODEL), F32),
            jax.ShapeDtypeStruct((t, ROUTE_W), F32),
        ],
        input_output_aliases=aliases,
        compiler_params=_cparams(("arbitrary",)),
        name="merge_p" if prompt else "merge_s",
    )(*args)


MOE_ROWS = 128
MOE_MAX_PASSES = N_ASSIGN // MOE_ROWS + N_EXPERTS
MOE_ISSUE_UNROLL = 8


def _moe_kernel(layer, pe_ref, first_ref, valid_ref, asg_ref, total_ref, wslot_ref, nxt_ref,
                h_hbm, wg_hbm, wu_hbm, wd_hbm, out_hbm,
                xbuf0, xbuf1, ybuf0, ybuf1, wg_f, wu_f, wd_f, wg_bf, wu_bf, wd_bf, sem_in, sem_out, sem_w):
    p = pl.program_id(0)
    total = total_ref[0]

    def weight_copies(e, ws):
        return [pltpu.make_async_copy(src.at[layer, e], dst.at[ws], sem_w.at[ws])
                for src, dst in ((wg_hbm, wg_f), (wu_hbm, wu_f), (wd_hbm, wd_f))]

    def load_expert_weights():
        ws = wslot_ref[p]

        @pl.when(p == 0)
        def _():
            for c in weight_copies(pe_ref[p], ws):
                c.start()

        @pl.when(nxt_ref[p] >= 0)
        def _():
            for c in weight_copies(nxt_ref[p], 1 - ws):
                c.start()

        for c in weight_copies(pe_ref[p], ws):
            c.wait()
        wg_bf[...] = wg_f[ws].astype(BF16)
        wu_bf[...] = wu_f[ws].astype(BF16)
        wd_bf[...] = wd_f[ws].astype(BF16)

    def gather_row(q_base, r, xb, sem):
        tok = asg_ref[q_base + r] >> 1
        return pltpu.make_async_copy(h_hbm.at[pl.ds(tok, 1), :], xb.at[pl.ds(r, 1), :], sem)

    def wait_gather(xb, sem):
        pltpu.make_async_copy(h_hbm.at[pl.ds(0, MOE_ROWS), :], xb, sem).wait()

    def scatter_row(q_base, r, yb, sem):
        a = asg_ref[q_base + r]
        return pltpu.make_async_copy(yb.at[pl.ds(r, 1), :], out_hbm.at[a & 1, pl.ds(a >> 1, 1), :], sem)

    def issue_scatter(q_base, yb, sem, n):
        def body8(i, c):
            for u in range(MOE_ISSUE_UNROLL):
                scatter_row(q_base, i * MOE_ISSUE_UNROLL + u, yb, sem).start(priority=u % 2)
            return c

        def body1(r, c):
            scatter_row(q_base, r, yb, sem).start()
            return c

        n8 = n // MOE_ISSUE_UNROLL
        lax.fori_loop(0, n8, body8, 0)
        lax.fori_loop(n8 * MOE_ISSUE_UNROLL, n, body1, 0)

    def wait_scatter(yb, sem, n):
        bit = MOE_ROWS
        while bit >= 1:
            @pl.when((n & bit) != 0)
            def _(bit=bit):
                pltpu.make_async_copy(yb.at[pl.ds(0, bit), :], out_hbm.at[0, pl.ds(0, bit), :], sem).wait()
            bit //= 2

    def run_pass(x_cur, x_nxt, y_cur, y_oth, s_cur, s_nxt):
        @pl.when(first_ref[p] == 1)
        def _():
            load_expert_weights()

        wait_gather(x_cur, sem_in.at[s_cur])
        nxt_base = (p + 1) * MOE_ROWS
        for r in range(MOE_ROWS):
            gather_row(nxt_base, r, x_nxt, sem_in.at[s_nxt]).start(priority=r % 2)

        x = x_cur[...].astype(BF16)
        gate = jnp.dot(x, wg_bf[...], preferred_element_type=F32)
        up = jnp.dot(x, wu_bf[...], preferred_element_type=F32)
        hid = (_silu(gate) * up).astype(BF16)
        y = jnp.dot(hid, wd_bf[...], preferred_element_type=F32)

        @pl.when(p >= 2)
        def _():
            wait_scatter(y_cur, sem_out.at[s_cur], valid_ref[jnp.maximum(p - 2, 0)])

        y_cur[...] = y
        issue_scatter(p * MOE_ROWS, y_cur, sem_out.at[s_cur], valid_ref[p])

        @pl.when(p == total - 1)
        def _():
            wait_gather(x_nxt, sem_in.at[s_nxt])

            @pl.when(p >= 1)
            def _():
                wait_scatter(y_oth, sem_out.at[s_nxt], valid_ref[jnp.maximum(p - 1, 0)])

            wait_scatter(y_cur, sem_out.at[s_cur], valid_ref[p])

    @pl.when((p == 0) & (total > 0))
    def _():
        def body(r, c):
            gather_row(0, r, xbuf0, sem_in.at[0]).start()
            return c

        lax.fori_loop(0, MOE_ROWS, body, 0, unroll=MOE_ISSUE_UNROLL)

    @pl.when((p < total) & (p % 2 == 0))
    def _():
        run_pass(xbuf0, xbuf1, ybuf0, ybuf1, 0, 1)

    @pl.when((p < total) & (p % 2 == 1))
    def _():
        run_pass(xbuf1, xbuf0, ybuf1, ybuf0, 1, 0)


def _moe_passes(eid):
    i32 = jnp.int32
    order = jnp.argsort(eid).astype(i32)
    count = jnp.sum((eid[:, None] == jnp.arange(N_EXPERTS, dtype=i32)[None, :]).astype(i32), axis=0)
    start = jnp.cumsum(count) - count
    npass = (count + MOE_ROWS - 1) // MOE_ROWS
    pend = jnp.cumsum(npass)
    total = pend[-1]
    last_e = jnp.max(jnp.where(count > 0, jnp.arange(N_EXPERTS, dtype=i32), 0))
    pidx = jnp.arange(MOE_MAX_PASSES + 1, dtype=i32)
    pe = jnp.minimum(jnp.searchsorted(pend, pidx, side="right").astype(i32), last_e)
    k_in = pidx - (pend - npass)[pe]
    valid = jnp.clip(count[pe] - k_in * MOE_ROWS, 0, MOE_ROWS)
    first = ((k_in == 0) & (pidx < total)).astype(i32)
    row0 = start[pe] + k_in * MOE_ROWS
    r = jnp.arange(MOE_ROWS, dtype=i32)[None, :]
    rows = jnp.where(r < valid[:, None], row0[:, None] + r, row0[:, None])
    asg = order[jnp.clip(rows, 0, N_ASSIGN - 1)].reshape(-1)
    used = count > 0
    ordinal = jnp.cumsum(used.astype(i32)) - 1
    wslot = (ordinal[pe] % 2).astype(i32)
    ids = jnp.where(used, jnp.arange(N_EXPERTS, dtype=i32), N_EXPERTS)
    after = jnp.concatenate([ids[1:], jnp.full((1,), N_EXPERTS, i32)])
    nxt_e = jnp.flip(lax.cummin(jnp.flip(after)))
    nxt = jnp.where(nxt_e[pe] < N_EXPERTS, nxt_e[pe], -1).astype(i32)
    return pe, first, valid.astype(i32), asg, total.reshape(1).astype(i32), wslot, nxt


def _moe(h2_all, eid, w_gate, w_up, w_down, layer):
    tables = _moe_passes(eid)
    hbm = pl.BlockSpec(memory_space=pl.ANY)
    grid_spec = pltpu.PrefetchScalarGridSpec(
        num_scalar_prefetch=len(tables),
        grid=(MOE_MAX_PASSES,),
        in_specs=[hbm, hbm, hbm, hbm],
        out_specs=hbm,
        scratch_shapes=[
            pltpu.VMEM((MOE_ROWS, D_MODEL), F32), pltpu.VMEM((MOE_ROWS, D_MODEL), F32),
            pltpu.VMEM((MOE_ROWS, D_MODEL), F32), pltpu.VMEM((MOE_ROWS, D_MODEL), F32),
            pltpu.VMEM((2, D_MODEL, D_EXPERT), F32),
            pltpu.VMEM((2, D_MODEL, D_EXPERT), F32),
            pltpu.VMEM((2, D_EXPERT, D_MODEL), F32),
            pltpu.VMEM((D_MODEL, D_EXPERT), BF16),
            pltpu.VMEM((D_MODEL, D_EXPERT), BF16),
            pltpu.VMEM((D_EXPERT, D_MODEL), BF16),
            pltpu.SemaphoreType.DMA((2,)),
            pltpu.SemaphoreType.DMA((2,)),
            pltpu.SemaphoreType.DMA((2,)),
        ],
    )
    return pl.pallas_call(
        functools.partial(_moe_kernel, layer),
        grid_spec=grid_spec,
        out_shape=jax.ShapeDtypeStruct((EXP_TOP_K, T_ALL, D_MODEL), F32),
        compiler_params=_cparams(("arbitrary",)),
        name="moe_experts",
    )(*tables, h2_all, w_gate, w_up, w_down)


def _final_kernel(x1_ref, o0_ref, o1_ref, rt_ref, gf_ref, g_ref, b_ref, y_ref):
    rt = rt_ref[...]
    ffn = rt[:, 2:3] * o0_ref[...] + rt[:, 3:4] * o1_ref[...]
    y_ref[...] = _ln(DN_ALPHA * x1_ref[...] + gf_ref[...] * ffn) * g_ref[...] + b_ref[...]


FINAL_TM = 512


def _final(x1, out2, route, mod, lw, layer, prompt):
    t = x1.shape[0]
    tm = FINAL_TM if prompt else DEC_BATCH
    blk0 = 0 if prompt else T_PROMPT // tm

    def const(shape):
        return pl.BlockSpec((None,) + shape, lambda i: (layer,) + (0,) * len(shape))

    return pl.pallas_call(
        _final_kernel,
        grid=(t // tm,),
        in_specs=[
            pl.BlockSpec((tm, D_MODEL), lambda i: (i, 0)),
            pl.BlockSpec((None, tm, D_MODEL), lambda i: (0, i + blk0, 0)),
            pl.BlockSpec((None, tm, D_MODEL), lambda i: (1, i + blk0, 0)),
            pl.BlockSpec((tm, ROUTE_W), lambda i: (i, 0)),
            _mod_spec(prompt, layer, MOD_G_F, tm),
            const((1, D_MODEL)), const((1, D_MODEL)),
        ],
        out_specs=pl.BlockSpec((tm, D_MODEL), lambda i: (i, 0)),
        out_shape=jax.ShapeDtypeStruct((t, D_MODEL), F32),
        compiler_params=_cparams(("arbitrary",)),
        name="final_p" if prompt else "final_s",
    )(x1, out2, out2, route, mod, lw["ln2_g"], lw["ln2_b"])


def _layer_params(i, conv_w, conv_b, dt_bias, a_log, d_skip, ssd_norm_w, v_ln_g, v_ln_b, w_s, b_s,
                  stacked):
    pad_h = LANES - SSD_HEADS
    cw, cb = conv_w[i], conv_b[i][None, :]
    lw = dict(stacked)
    lw.update(
        cwx=cw[:, :SSD_INNER], cwb=cw[:, SSD_INNER:SSD_INNER + BC_W], cwc=cw[:, SSD_INNER + BC_W:],
        cbx=cb[:, :SSD_INNER], cbb=cb[:, SSD_INNER:SSD_INNER + BC_W], cbc=cb[:, SSD_INNER + BC_W:],
        dtb=jnp.pad(dt_bias[i], (0, pad_h))[None, :],
        alog=jnp.pad(a_log[i], (0, pad_h))[None, :],
        dskip=jnp.repeat(d_skip[i], SSD_HEAD_DIM)[None, :],
        nw=ssd_norm_w[i][None, :],
        vg=v_ln_g[i][None, :], vb=v_ln_b[i][None, :],
        ws=w_s[i], bst=b_s[i].T,
        ws0=jnp.repeat(w_s[i, :, 0, 0], SG_GDIM)[None, :],
        bs0=jnp.repeat(b_s[i, :, 0], SG_GDIM)[None, :],
    )
    return lw


def kernel(x_prompt, x_sample, state_ssd, state_conv, c_prompt, c_sample, w_ada, b_ada, w_in, conv_w, conv_b,
           dt_bias, a_log, d_skip, ssd_norm_w, v_ln_g, v_ln_b, w_s, b_s, w_pa, w_pb, w_o, ln1_g, ln1_b,
           w_rg, b_rg, w_re, b_re, w_gate, w_up, w_down, ln2_g, ln2_b):
    xp = x_prompt.reshape(T_PROMPT, D_MODEL)
    xs = x_sample.reshape(DEC_BATCH, D_MODEL)

    n_c = BATCH + DEC_BATCH
    c_all = jnp.pad(jnp.concatenate([c_sample, c_prompt], axis=0), ((0, (-n_c) % 8), (0, 0)))
    mod_s = _ada_mod(c_all, w_ada, b_ada)
    mod_p = mod_s[:, DEC_BATCH:n_c].reshape(DEPTH, BATCH, 1, 6 * D_MODEL)

    w_in_p = jnp.swapaxes(w_in, 1, 2)
    r_pad = ROUTE_W - N_EXP_GROUPS - N_EXPERTS
    stacked = dict(
        w_pa=w_pa.astype(BF16), w_pb=w_pb.astype(BF16), w_o=w_o.astype(BF16),
        ln1_g=ln1_g[:, None, :], ln1_b=ln1_b[:, None, :], ln2_g=ln2_g[:, None, :], ln2_b=ln2_b[:, None, :],
        w_r=jnp.pad(jnp.concatenate([w_rg, w_re], axis=-1), ((0, 0), (0, 0), (0, r_pad))).astype(BF16),
        b_r=jnp.pad(jnp.concatenate([b_rg, b_re], axis=-1), ((0, 0), (0, r_pad)))[:, None, :],
    )
    conv_state = state_conv.reshape(DEPTH, DEC_BATCH, (CONV_W - 1) * CONV_CH)
    state_all = state_ssd.reshape(DEPTH, DEC_BATCH, SSD_INNER, SSD_STATE)

    ssd_p, conv_p, conv_s, v_s = [], [], [], []
    ssd_s = None
    for i in range(DEPTH):
        lw = _layer_params(i, conv_w, conv_b, dt_bias, a_log, d_skip, ssd_norm_w, v_ln_g, v_ln_b, w_s, b_s,
                           stacked)
        proj_p = _inproj(xp, mod_p, w_in_p, i, True, 1024)
        proj_s = _inproj(xs, mod_s, w_in_p, i, False, DEC_BATCH)
        ya_p, yb_p, st_p, tail_p = _mix_prompt(proj_p, lw, i)
        xa, bm, cm, dec, dtx, yb_s, v_rows, ncs = _sprep(proj_s, conv_state, lw, i)
        ssd_s, ya_s = _sstate(state_all, ssd_s, (xa, bm, cm, dec, dtx), proj_s, lw, i)
        x1_p, h2_all, rt_p = _merge(ya_p, yb_p, proj_p, xp, mod_p, lw, i, True, None)
        x1_s, h2_all, rt_s = _merge(ya_s, yb_s, proj_s, xs, mod_s, lw, i, False, h2_all)
        route = jnp.concatenate([rt_p, rt_s], axis=0)
        eid = route[:, :EXP_TOP_K].astype(jnp.int32).reshape(-1)
        out2 = _moe(h2_all, eid, w_gate, w_up, w_down, i)
        xp = _final(x1_p, out2, rt_p, mod_p, lw, i, True)
        xs = _final(x1_s, out2, rt_s, mod_s, lw, i, False)

        ssd_p.append(st_p.reshape(BATCH, SSD_HEADS, SSD_HEAD_DIM, SSD_STATE))
        conv_p.append(tail_p)
        conv_s.append(ncs.reshape(DEC_BATCH, CONV_W - 1, CONV_CH))
        v_s.append(v_rows.reshape(DEC_BATCH, 1, SG_WIDTH))

    return (xp.reshape(BATCH, SEQ, D_MODEL), xs.reshape(DEC_BATCH, 1, D_MODEL),
            jnp.stack(ssd_p), jnp.stack(conv_p),
            ssd_s.reshape(DEPTH, DEC_BATCH, SSD_HEADS, SSD_HEAD_DIM, SSD_STATE),
            jnp.stack(conv_s), jnp.stack(v_s))
```

```python
import functools
import math

import numpy as np
import jax
import jax.numpy as jnp
from jax import lax
from jax.experimental import pallas as pl
from jax.experimental.pallas import tpu as pltpu

F32 = jnp.float32
BF16 = jnp.bfloat16

D_MODEL = 2048
BATCH = 4
SEQ = 2048
DEPTH = 4
DEC_BATCH = 128
SSD_HEAD_DIM = 64
SSD_INNER = D_MODEL
SSD_HEADS = SSD_INNER // SSD_HEAD_DIM
SSD_GROUPS = 4
SSD_HPG = SSD_HEADS // SSD_GROUPS
SSD_STATE = 128
CONV_W = 4
BC_W = SSD_GROUPS * SSD_STATE
CONV_CH = SSD_INNER + 2 * BC_W
SG_CHUNK = 128
SG_WIDTH = D_MODEL
SG_GROUPS = 8
SG_GDIM = SG_WIDTH // SG_GROUPS
N_EXP_GROUPS = 8
EXP_PER_GROUP = 8
N_EXPERTS = 64
EXP_TOP_K = 2
D_EXPERT = D_MODEL // 4
DN_ALPHA = (2.0 * DEPTH) ** 0.25
NORM_EPS = 1e-5

T_PROMPT = BATCH * SEQ
T_ALL = T_PROMPT + DEC_BATCH
N_ASSIGN = T_ALL * EXP_TOP_K

LANES = 128
CHUNK = 128
VMEM_LIMIT = 56 * 1024 * 1024

SEG_Z = 0
SEG_XS = 2048
SEG_U = 4096
SEG_V = 6144
SEG_GA = 8192
SEG_GB = 10240
SEG_B = 12288
SEG_C = 12800
SEG_DT = 13312
PROJ_W = 13824
PROJ_TN = 512
ROUTE_W = LANES

SQRT_HALF = np.float32(np.sqrt(0.5))
NEG_BIG = -3.0e38


def _cparams(sem):
    return pltpu.CompilerParams(dimension_semantics=sem, vmem_limit_bytes=VMEM_LIMIT)


def _silu(x):
    return x * jax.nn.sigmoid(x)


def _gelu(x):
    return 0.5 * x * (1.0 + lax.erf(x * SQRT_HALF))


def _softplus(x):
    return jnp.maximum(x, 0.0) + jnp.log1p(jnp.exp(-jnp.abs(x)))


def _ln(x):
    mu = jnp.mean(x, axis=-1, keepdims=True)
    xc = x - mu
    var = jnp.mean(xc * xc, axis=-1, keepdims=True)
    return xc * lax.rsqrt(var + NORM_EPS)


ADA_TN = 1024


def _ada_kernel(c_ref, w_ref, b_ref, o_ref):
    a = _silu(c_ref[...]).astype(BF16)
    o_ref[...] = jnp.dot(a, w_ref[...].astype(BF16), preferred_element_type=F32) + b_ref[...]


def _ada_mod(c_all, w_ada, b_ada):
    rows = c_all.shape[0]
    n = 6 * D_MODEL
    return pl.pallas_call(
        _ada_kernel,
        grid=(DEPTH, n // ADA_TN),
        in_specs=[
            pl.BlockSpec((rows, D_MODEL), lambda l, j: (0, 0)),
            pl.BlockSpec((None, D_MODEL, ADA_TN), lambda l, j: (l, 0, j)),
            pl.BlockSpec((None, 1, ADA_TN), lambda l, j: (l, 0, j)),
        ],
        out_specs=pl.BlockSpec((None, rows, ADA_TN), lambda l, j: (l, 0, j)),
        out_shape=jax.ShapeDtypeStruct((DEPTH, rows, n), F32),
        compiler_params=_cparams(("arbitrary", "arbitrary")),
        name="ada_mod",
    )(c_all, w_ada, b_ada.reshape(DEPTH, 1, n))


MOD_SH_M, MOD_SC_M, MOD_G_M, MOD_SH_F, MOD_SC_F, MOD_G_F = range(6)


def _mod_spec(prompt, layer, k, tm):
    if prompt:
        return pl.BlockSpec((None, None, 1, D_MODEL), lambda i, *_: (layer, (i * tm) // SEQ, 0, k))
    return pl.BlockSpec((None, tm, D_MODEL), lambda i, *_: (layer, 0, k))


INPROJ_RC = 512
IN_COLS = SSD_INNER + CONV_CH + SSD_HEADS + 2 * SG_WIDTH + 2 * D_MODEL
SRC_DT = SSD_INNER + CONV_CH


def _inproj_kernel(x_ref, sc_ref, sh_ref, w_ref, o_ref, h_ref, w_bf):
    j = pl.program_id(1)

    @pl.when(j == 0)
    def _():
        h = _ln(x_ref[...]) * (1.0 + sc_ref[...]) + sh_ref[...]
        h_ref[...] = h.astype(BF16)

    col = j * PROJ_TN
    tm = h_ref.shape[0]
    rc = min(tm, INPROJ_RC)

    def emit(act):
        w_bf[...] = w_ref[0].astype(BF16)
        for r in range(tm // rc):
            rows = slice(r * rc, (r + 1) * rc)
            o_ref[rows, :] = act(lax.dot_general(h_ref[rows, :], w_bf[...], (((1,), (1,)), ((), ())),
                                                 preferred_element_type=F32))

    @pl.when(col < SEG_XS)
    def _():
        emit(_silu)

    @pl.when((col >= SEG_U) & (col < SEG_GA))
    def _():
        emit(_gelu)

    @pl.when((col >= SEG_GA) & (col < SEG_B))
    def _():
        emit(jax.nn.sigmoid)

    @pl.when(((col >= SEG_XS) & (col < SEG_U)) | (col >= SEG_B))
    def _():
        emit(lambda a: a)


def _w_in_row(j):
    t_u, t_b, t_dt = SEG_U // PROJ_TN, SEG_B // PROJ_TN, SEG_DT // PROJ_TN
    return jnp.where(j < t_u, j * PROJ_TN,
                     jnp.where(j < t_b, SRC_DT + SSD_HEADS + (j - t_u) * PROJ_TN,
                               jnp.where(j < t_dt, 2 * SSD_INNER + (j - t_b) * PROJ_TN, SRC_DT)))


def _inproj(x, mod, w_in_t, layer, prompt, tm):
    t = x.shape[0]
    return pl.pallas_call(
        _inproj_kernel,
        grid=(t // tm, PROJ_W // PROJ_TN),
        in_specs=[
            pl.BlockSpec((tm, D_MODEL), lambda i, j: (i, 0)),
            _mod_spec(prompt, layer, MOD_SC_M, tm),
            _mod_spec(prompt, layer, MOD_SH_M, tm),
            pl.BlockSpec((pl.Element(1), pl.Element(PROJ_TN), pl.Element(D_MODEL)),
                         lambda i, j: (layer, pl.multiple_of(_w_in_row(j), 8), 0)),
        ],
        out_specs=pl.BlockSpec((tm, PROJ_TN), lambda i, j: (i, j)),
        out_shape=jax.ShapeDtypeStruct((t, PROJ_W), F32),
        scratch_shapes=[pltpu.VMEM((tm, D_MODEL), BF16), pltpu.VMEM((PROJ_TN, D_MODEL), BF16)],
        compiler_params=_cparams(("arbitrary", "arbitrary")),
        name="inproj_p" if prompt else "inproj_s",
    )(x, mod, mod, w_in_t)


def _conv_chunk(ext_ref, raw, w_ref, b_ref):
    ext_ref[8:8 + CHUNK, :] = raw
    acc = b_ref[...] + w_ref[CONV_W - 1:CONV_W, :] * raw
    for k in range(CONV_W - 1):
        acc = acc + w_ref[k:k + 1, :] * ext_ref[5 + k:5 + k + CHUNK, :]
    ext_ref[5:8, :] = ext_ref[5 + CHUNK:8 + CHUNK, :]
    return acc


def _group_rmsnorm(y, norm_w):
    gw = SSD_INNER // SSD_GROUPS
    outs = []
    for g in range(SSD_GROUPS):
        yg = y[:, g * gw:(g + 1) * gw]
        ms = jnp.mean(yg * yg, axis=-1, keepdims=True)
        outs.append(yg * lax.rsqrt(ms + NORM_EPS))
    return jnp.concatenate(outs, axis=-1) * norm_w


def _mix_prompt_kernel(z_ref, xs_ref, b_ref, c_ref, dt_ref, u_ref, v_ref,
                       cwx_ref, cwb_ref, cwc_ref, cbx_ref, cbb_ref, cbc_ref,
                       dtb_ref, alog_ref, dskip_ref, nw_ref,
                       vg_ref, vb_ref, ws_ref, bst_ref,
                       ya_ref, yb_ref, st_ref, tail_ref,
                       ex_ref, eb_ref, ec_ref, st_t_ref, y_ref):
    c = pl.program_id(1)
    n_chunks = pl.num_programs(1)

    @pl.when(c == 0)
    def _():
        ex_ref[0:8, :] = jnp.zeros((8, SSD_INNER), F32)
        eb_ref[0:8, :] = jnp.zeros((8, BC_W), F32)
        ec_ref[0:8, :] = jnp.zeros((8, BC_W), F32)
        st_t_ref[...] = jnp.zeros_like(st_t_ref)

    raw_x = xs_ref[...]
    raw_b = b_ref[...]
    raw_c = c_ref[...]

    @pl.when(c == n_chunks - 1)
    def _():
        tail_ref[:, 0:SSD_INNER] = raw_x[CHUNK - 3:CHUNK, :]
        tail_ref[:, SSD_INNER:SSD_INNER + BC_W] = raw_b[CHUNK - 3:CHUNK, :]
        tail_ref[:, SSD_INNER + BC_W:CONV_CH] = raw_c[CHUNK - 3:CHUNK, :]

    xs = _silu(_conv_chunk(ex_ref, raw_x, cwx_ref, cbx_ref))
    bm = _silu(_conv_chunk(eb_ref, raw_b, cwb_ref, cbb_ref))
    cm = _silu(_conv_chunk(ec_ref, raw_c, cwc_ref, cbc_ref))

    dt = _softplus(dt_ref[...] + dtb_ref[...])
    a = -jnp.exp(alog_ref[...])
    d_a = dt * a
    rows = lax.broadcasted_iota(jnp.int32, (CHUNK, CHUNK), 0)
    cols = lax.broadcasted_iota(jnp.int32, (CHUNK, CHUNK), 1)
    causal = rows >= cols
    a_cum = jnp.dot(causal.astype(F32), d_a, preferred_element_type=F32,
                    precision=lax.Precision.HIGHEST)
    a_cum_t = a_cum.T
    dt_t = dt.T
    a_last_col = a_cum_t[:, CHUNK - 1:CHUNK]
    exp_a_cum = jnp.exp(a_cum)
    w_t = dt_t * jnp.exp(a_last_col - a_cum_t)
    ea_col = jnp.exp(a_last_col)
    lo = cols < SSD_HEAD_DIM

    for g in range(SSD_GROUPS):
        bg = bm[:, g * SSD_STATE:(g + 1) * SSD_STATE]
        cg = cm[:, g * SSD_STATE:(g + 1) * SSD_STATE]
        cb = lax.dot_general(cg.astype(BF16), bg.astype(BF16), (((1,), (1,)), ((), ())),
                             preferred_element_type=F32)
        bg_t = bg.T
        for q in range(SSD_HPG // 2):
            e0 = g * SSD_HPG + 2 * q
            sl = slice((e0 // 2) * LANES, (e0 // 2 + 1) * LANES)
            lhs = []
            upd = []
            for e in (e0, e0 + 1):
                seg = a_cum[:, e:e + 1] - a_cum_t[e:e + 1, :]
                dec = jnp.where(causal, jnp.exp(jnp.where(causal, seg, 0.0)), 0.0)
                lhs.append((cb * dec * dt_t[e:e + 1, :]).astype(BF16))
                lhs.append((cg * exp_a_cum[:, e:e + 1]).astype(BF16))
                upd.append((bg_t * w_t[e:e + 1, :]).astype(BF16))
            xs_p = xs[:, sl]
            st_p = st_t_ref[:, sl]
            xs_lo = jnp.where(lo, xs_p, 0.0).astype(BF16)
            xs_hi = jnp.where(lo, 0.0, xs_p).astype(BF16)
            st_lo = jnp.where(lo, st_p, 0.0).astype(BF16)
            st_hi = jnp.where(lo, 0.0, st_p).astype(BF16)
            y_ref[:, sl] = jnp.dot(jnp.concatenate(lhs, axis=1),
                                   jnp.concatenate([xs_lo, st_lo, xs_hi, st_hi], axis=0),
                                   preferred_element_type=F32)
            dec_p = jnp.where(lo, ea_col[e0:e0 + 1, :], ea_col[e0 + 1:e0 + 2, :])
            st_t_ref[:, sl] = st_p * dec_p + jnp.dot(jnp.concatenate(upd, axis=1),
                                                     jnp.concatenate([xs_lo, xs_hi], axis=0),
                                                     preferred_element_type=F32)

    y = (y_ref[...] + dskip_ref[...] * xs) * z_ref[...]
    ya_ref[...] = _group_rmsnorm(y, nw_ref[...]).astype(BF16)

    @pl.when(c == n_chunks - 1)
    def _():
        st_ref[...] = st_t_ref[...].T

    v = (_ln(v_ref[...]) * vg_ref[...] + vb_ref[...]).astype(BF16)
    for g in range(SG_GROUPS):
        sl = slice(g * SG_GDIM, (g + 1) * SG_GDIM)
        w = jnp.where(causal, ws_ref[g], 0.0).astype(BF16)
        s = jnp.dot(w, v[:, sl], preferred_element_type=F32) + bst_ref[:, g:g + 1]
        yb_ref[:, sl] = (u_ref[:, sl] * s).astype(BF16)


def _mix_prompt(proj, lw, layer):
    n_chunks = SEQ // CHUNK

    def pspec(off, width):
        return pl.BlockSpec((CHUNK, width), lambda b, c: (b * n_chunks + c, off // width))

    def full(arr):
        return pl.BlockSpec(arr.shape, lambda b, c: (0,) * arr.ndim)

    params = [lw["cwx"], lw["cwb"], lw["cwc"], lw["cbx"], lw["cbb"], lw["cbc"],
              lw["dtb"], lw["alog"], lw["dskip"], lw["nw"],
              lw["vg"], lw["vb"], lw["ws"], lw["bst"]]
    return pl.pallas_call(
        _mix_prompt_kernel,
        grid=(BATCH, n_chunks),
        in_specs=[pspec(SEG_Z, SSD_INNER), pspec(SEG_XS, SSD_INNER), pspec(SEG_B, BC_W), pspec(SEG_C, BC_W),
                  pspec(SEG_DT, LANES), pspec(SEG_U, SG_WIDTH), pspec(SEG_V, SG_WIDTH)]
        + [full(p) for p in params],
        out_specs=[
            pl.BlockSpec((CHUNK, SSD_INNER), lambda b, c: (b * n_chunks + c, 0)),
            pl.BlockSpec((CHUNK, SG_WIDTH), lambda b, c: (b * n_chunks + c, 0)),
            pl.BlockSpec((None, SSD_INNER, SSD_STATE), lambda b, c: (b, 0, 0)),
            pl.BlockSpec((None, CONV_W - 1, CONV_CH), lambda b, c: (b, 0, 0)),
        ],
        out_shape=[
            jax.ShapeDtypeStruct((T_PROMPT, SSD_INNER), BF16),
            jax.ShapeDtypeStruct((T_PROMPT, SG_WIDTH), BF16),
            jax.ShapeDtypeStruct((BATCH, SSD_INNER, SSD_STATE), F32),
            jax.ShapeDtypeStruct((BATCH, CONV_W - 1, CONV_CH), F32),
        ],
        scratch_shapes=[
            pltpu.VMEM((CHUNK + 8, SSD_INNER), F32),
            pltpu.VMEM((CHUNK + 8, BC_W), F32),
            pltpu.VMEM((CHUNK + 8, BC_W), F32),
            pltpu.VMEM((SSD_STATE, SSD_INNER), F32),
            pltpu.VMEM((CHUNK, SSD_INNER), F32),
        ],
        compiler_params=_cparams(("arbitrary", "arbitrary")),
        name="mix_prompt",
    )(proj, proj, proj, proj, proj, proj, proj, *params)


def _sprep_kernel(xs_ref, b_ref, c_ref, dt_ref, u_ref, v_ref, cs_ref,
                  cwx_ref, cwb_ref, cwc_ref, cbx_ref, cbb_ref, cbc_ref,
                  dtb_ref, alog_ref, vg_ref, vb_ref, ws0_ref, bs0_ref,
                  xa_ref, bm_ref, cm_ref, dec_ref, dtx_ref, yb_ref, vo_ref, ncs_ref):
    raw = (xs_ref[...], b_ref[...], c_ref[...])
    offs = (0, SSD_INNER, SSD_INNER + BC_W)
    widths = (SSD_INNER, BC_W, BC_W)
    cws = (cwx_ref, cwb_ref, cwc_ref)
    cbs = (cbx_ref, cbb_ref, cbc_ref)
    act = []
    for r, off, wd, cw, cbias in zip(raw, offs, widths, cws, cbs):
        acc = cbias[...] + cw[CONV_W - 1:CONV_W, :] * r
        for k in range(CONV_W - 1):
            acc = acc + cw[k:k + 1, :] * cs_ref[:, k * CONV_CH + off:k * CONV_CH + off + wd]
        act.append(_silu(acc))
        ncs_ref[:, (CONV_W - 2) * CONV_CH + off:(CONV_W - 2) * CONV_CH + off + wd] = r
    ncs_ref[:, 0:(CONV_W - 2) * CONV_CH] = cs_ref[:, CONV_CH:(CONV_W - 1) * CONV_CH]
    xa, bm, cm = act
    xa_ref[...] = xa
    bm_ref[...] = bm
    cm_ref[...] = cm

    dt = _softplus(dt_ref[...] + dtb_ref[...])
    a = -jnp.exp(alog_ref[...])
    dec_ref[...] = jnp.exp(dt * a).T
    dt_t = dt.T
    for p in range(SSD_HEADS // 2):
        xa_t = xa[:, p * LANES:(p + 1) * LANES].T
        for half in range(2):
            h = 2 * p + half
            rs = slice(h * SSD_HEAD_DIM, (h + 1) * SSD_HEAD_DIM)
            dtx_ref[rs, :] = xa_t[half * SSD_HEAD_DIM:(half + 1) * SSD_HEAD_DIM, :] * dt_t[h:h + 1, :]

    v = _ln(v_ref[...]) * vg_ref[...] + vb_ref[...]
    vo_ref[...] = v
    yb_ref[...] = (u_ref[...] * (ws0_ref[...] * v + bs0_ref[...])).astype(BF16)


def _sprep(proj, conv_state, lw, layer):
    def pspec(off, width):
        return pl.BlockSpec((DEC_BATCH, width), lambda i: (0, off // width))

    def full(arr):
        return pl.BlockSpec(arr.shape, lambda i: (0,) * arr.ndim)

    params = [lw["cwx"], lw["cwb"], lw["cwc"], lw["cbx"], lw["cbb"], lw["cbc"],
              lw["dtb"], lw["alog"], lw["vg"], lw["vb"], lw["ws0"], lw["bs0"]]
    cw = (CONV_W - 1) * CONV_CH
    outs = [
        ((DEC_BATCH, SSD_INNER), F32), ((DEC_BATCH, BC_W), F32), ((DEC_BATCH, BC_W), F32),
        ((LANES, DEC_BATCH), F32), ((SSD_INNER, DEC_BATCH), F32),
        ((DEC_BATCH, SG_WIDTH), BF16), ((DEC_BATCH, SG_WIDTH), F32), ((DEC_BATCH, cw), F32),
    ]
    return pl.pallas_call(
        _sprep_kernel,
        grid=(1,),
        in_specs=[pspec(SEG_XS, SSD_INNER), pspec(SEG_B, BC_W), pspec(SEG_C, BC_W), pspec(SEG_DT, LANES),
                  pspec(SEG_U, SG_WIDTH), pspec(SEG_V, SG_WIDTH),
                  pl.BlockSpec((None, DEC_BATCH, cw), lambda i: (layer, 0, 0))]
        + [full(p) for p in params],
        out_specs=[pl.BlockSpec(s, lambda i: (0, 0)) for s, _ in outs],
        out_shape=[jax.ShapeDtypeStruct(s, d) for s, d in outs],
        compiler_params=_cparams(("arbitrary",)),
        name="sample_prep",
    )(proj, proj, proj, proj, proj, proj, conv_state, *params)


SSTATE_BT = 8


def _sstate_kernel(st_ref, dec_ref, dtx_ref, bm_ref, cm_ref, xa_ref, z_ref, dskip_ref, nw_ref,
                   *rest):
    sto_ref, ya_ref, yt_ref, ycol_ref = rest[-4:]
    i = pl.program_id(0)
    lane = lax.broadcasted_iota(jnp.int32, (SSD_INNER, DEC_BATCH), 1)

    @pl.when(i == 0)
    def _():
        yt_ref[...] = jnp.zeros_like(yt_ref)

    base = i * SSTATE_BT
    shift = (DEC_BATCH - base) % DEC_BATCH
    dec_r = pltpu.roll(dec_ref[...], shift, 1)
    dtx_r = pltpu.roll(dtx_ref[...], shift, 1)
    ycol_ref[...] = jnp.zeros_like(ycol_ref)
    gr = SSD_HPG * SSD_HEAD_DIM
    for j in range(SSTATE_BT):
        b_row = bm_ref[pl.ds(base + j, 1), :]
        c_row = cm_ref[pl.ds(base + j, 1), :]
        for g in range(SSD_GROUPS):
            rs = slice(g * gr, (g + 1) * gr)
            ns = slice(g * SSD_STATE, (g + 1) * SSD_STATE)
            dec_g = jnp.concatenate(
                [jnp.broadcast_to(dec_r[h:h + 1, j:j + 1], (SSD_HEAD_DIM, SSD_STATE))
                 for h in range(g * SSD_HPG, (g + 1) * SSD_HPG)], axis=0)
            hn = st_ref[j, rs, :] * dec_g + dtx_r[rs, j:j + 1] * b_row[:, ns]
            sto_ref[j, rs, :] = hn
            ycol_ref[rs, j:j + 1] = jnp.sum(hn * c_row[:, ns], axis=1, keepdims=True)
    put = (lane >= base) & (lane < base + SSTATE_BT)
    yt_ref[...] = jnp.where(put, pltpu.roll(ycol_ref[...], base, 1), yt_ref[...])

    @pl.when(i == pl.num_programs(0) - 1)
    def _():
        xa = xa_ref[...]
        cols = []
        for k in range(SSD_INNER // LANES):
            cols.append(yt_ref[k * LANES:(k + 1) * LANES, :].T)
        y = jnp.concatenate(cols, axis=1)
        y = (y + dskip_ref[...] * xa) * z_ref[...]
        ya_ref[...] = _group_rmsnorm(y, nw_ref[...]).astype(BF16)


def _sstate(state_all, stacked_prev, prep, proj, lw, layer):
    xa, bm, cm, dec, dtx = prep

    def full(arr):
        return pl.BlockSpec(arr.shape, lambda i: (0,) * arr.ndim)

    in_specs = [
        pl.BlockSpec((None, SSTATE_BT, SSD_INNER, SSD_STATE), lambda i: (layer, i, 0, 0)),
        full(dec), full(dtx), full(bm), full(cm), full(xa),
        pl.BlockSpec((DEC_BATCH, SSD_INNER), lambda i: (0, SEG_Z // SSD_INNER)),
        full(lw["dskip"]), full(lw["nw"]),
    ]
    args = [state_all, dec, dtx, bm, cm, xa, proj, lw["dskip"], lw["nw"]]
    aliases = {}
    if stacked_prev is not None:
        in_specs.append(pl.BlockSpec(memory_space=pl.ANY))
        args.append(stacked_prev)
        aliases = {len(args) - 1: 0}
    return pl.pallas_call(
        _sstate_kernel,
        grid=(DEC_BATCH // SSTATE_BT,),
        in_specs=in_specs,
        out_specs=[
            pl.BlockSpec((None, SSTATE_BT, SSD_INNER, SSD_STATE), lambda i: (layer, i, 0, 0)),
            pl.BlockSpec((DEC_BATCH, SSD_INNER), lambda i: (0, 0)),
        ],
        out_shape=[
            jax.ShapeDtypeStruct((DEPTH, DEC_BATCH, SSD_INNER, SSD_STATE), F32),
            jax.ShapeDtypeStruct((DEC_BATCH, SSD_INNER), BF16),
        ],
        scratch_shapes=[pltpu.VMEM((SSD_INNER, DEC_BATCH), F32), pltpu.VMEM((SSD_INNER, DEC_BATCH), F32)],
        input_output_aliases=aliases,
        compiler_params=_cparams(("arbitrary",)),
        name="sample_state",
    )(*args)


def _merge_kernel(ya_ref, yb_ref, ga_ref, gb_ref, x_ref, gm_ref, scf_ref, shf_ref,
                  wpa_ref, wpb_ref, wo_ref, l1g_ref, l1b_ref, wr_ref, br_ref, *rest):
    x1_ref, h2_ref, rt_ref = rest[-3:]
    g_m, sc_f, sh_f = gm_ref[...], scf_ref[...], shf_ref[...]
    t1 = jnp.dot(ya_ref[...], wpa_ref[...], preferred_element_type=F32)
    t2 = jnp.dot(yb_ref[...], wpb_ref[...], preferred_element_type=F32)
    m = (ga_ref[...] * t1 + gb_ref[...] * t2).astype(BF16)
    mix = jnp.dot(m, wo_ref[...], preferred_element_type=F32)
    x1 = _ln(DN_ALPHA * x_ref[...] + g_m * mix) * l1g_ref[...] + l1b_ref[...]
    x1_ref[...] = x1
    h2 = _ln(x1) * (1.0 + sc_f) + sh_f
    h2_ref[...] = h2

    rl = jnp.dot(h2.astype(BF16), wr_ref[...], preferred_element_type=F32) + br_ref[...]
    lane = lax.broadcasted_iota(jnp.int32, rl.shape, 1)
    lane_f = lane.astype(F32)
    is_g = lane < N_EXP_GROUPS
    lg = jnp.where(is_g, rl, NEG_BIG)
    gmax = jnp.max(lg, axis=-1, keepdims=True)
    gidx = jnp.min(jnp.where(lg == gmax, lane_f, float(LANES)), axis=-1, keepdims=True)
    p_grp = 1.0 / jnp.sum(jnp.where(is_g, jnp.exp(lg - gmax), 0.0), axis=-1, keepdims=True)
    e_lane = lane - N_EXP_GROUPS
    in_grp = (e_lane >= 0) & (e_lane < N_EXPERTS) & (lax.shift_right_arithmetic(e_lane, 3).astype(F32) == gidx)
    le = jnp.where(in_grp, rl, NEG_BIG)
    m1 = jnp.max(le, axis=-1, keepdims=True)
    i1 = jnp.min(jnp.where(le == m1, lane_f, float(LANES)), axis=-1, keepdims=True)
    le2 = jnp.where(lane_f == i1, NEG_BIG, le)
    m2 = jnp.max(le2, axis=-1, keepdims=True)
    i2 = jnp.min(jnp.where(le2 == m2, lane_f, float(LANES)), axis=-1, keepdims=True)
    e2 = jnp.exp(m2 - m1)
    den = 1.0 + e2
    w1 = (1.0 / den) * p_grp
    w2 = (e2 / den) * p_grp
    rt_ref[...] = jnp.where(lane == 0, i1 - N_EXP_GROUPS,
                            jnp.where(lane == 1, i2 - N_EXP_GROUPS,
                                      jnp.where(lane == 2, w1, jnp.where(lane == 3, w2, 0.0))))


MERGE_TM = 256


def _merge(ya, yb, proj, x, mod, lw, layer, prompt, h2_prev):
    t = x.shape[0]
    tm = MERGE_TM if prompt else DEC_BATCH

    def const(shape):
        return pl.BlockSpec((None,) + shape, lambda i: (layer,) + (0,) * len(shape),
                            pipeline_mode=pl.Buffered(1))

    def gspec(off):
        return pl.BlockSpec((tm, D_MODEL), lambda i: (i, off // D_MODEL))

    in_specs = [
        pl.BlockSpec((tm, D_MODEL), lambda i: (i, 0)),
        pl.BlockSpec((tm, D_MODEL), lambda i: (i, 0)),
        gspec(SEG_GA), gspec(SEG_GB),
        pl.BlockSpec((tm, D_MODEL), lambda i: (i, 0)),
        _mod_spec(prompt, layer, MOD_G_M, tm),
        _mod_spec(prompt, layer, MOD_SC_F, tm),
        _mod_spec(prompt, layer, MOD_SH_F, tm),
        const((D_MODEL, D_MODEL)), const((D_MODEL, D_MODEL)), const((D_MODEL, D_MODEL)),
        const((1, D_MODEL)), const((1, D_MODEL)),
        const((D_MODEL, ROUTE_W)), const((1, ROUTE_W)),
    ]
    args = [ya, yb, proj, proj, x, mod, mod, mod, lw["w_pa"], lw["w_pb"], lw["w_o"],
            lw["ln1_g"], lw["ln1_b"], lw["w_r"], lw["b_r"]]
    aliases = {}
    h2_blk = 0
    if h2_prev is not None:
        in_specs.append(pl.BlockSpec(memory_space=pl.ANY))
        args.append(h2_prev)
        aliases = {len(args) - 1: 1}
        h2_blk = T_PROMPT // tm
    return pl.pallas_call(
        _merge_kernel,
        grid=(t // tm,),
        in_specs=in_specs,
        out_specs=[
            pl.BlockSpec((tm, D_MODEL), lambda i: (i, 0)),
            pl.BlockSpec((tm, D_MODEL), lambda i: (i + h2_blk, 0)),
            pl.BlockSpec((tm, ROUTE_W), lambda i: (i, 0)),
        ],
        out_shape=[
            jax.ShapeDtypeStruct((t, D_MODEL), F32),
            jax.ShapeDtypeStruct((T_ALL, D_MODEL), F32),
            jax.ShapeDtypeStruct((t, ROUTE_W), F32),
        ],
        input_output_aliases=aliases,
        compiler_params=_cparams(("arbitrary",)),
        name="merge_p" if prompt else "merge_s",
    )(*args)


MOE_ROWS = 128
MOE_MAX_PASSES = N_ASSIGN // MOE_ROWS + N_EXPERTS
MOE_ISSUE_UNROLL = 8
MOE_W_AHEAD = 1
MOE_W_BUFS = MOE_W_AHEAD + 1


def _moe_kernel(layer, pe_ref, first_ref, valid_ref, row0_ref, order_ref, total_ref, wslot_ref, nxt_ref,
                h_hbm, wg_hbm, wu_hbm, wd_hbm, out_hbm,
                xbuf0, xbuf1, ybuf0, ybuf1, wg_f, wu_f, wd_f, wg_bf, wu_bf, wd_bf, sem_in, sem_out, sem_w):
    p = pl.program_id(0)
    total = total_ref[0]

    def weight_copies(e, ws):
        return [pltpu.make_async_copy(src.at[layer, e], dst.at[ws], sem_w.at[ws])
                for src, dst in ((wg_hbm, wg_f), (wu_hbm, wu_f), (wd_hbm, wd_f))]

    def load_expert_weights():
        ws = wslot_ref[p]

        @pl.when(p == 0)
        def _():
            for c in weight_copies(pe_ref[p], ws):
                c.start(priority=1)
            for k in range(1, MOE_W_AHEAD):
                e_k = nxt_ref[(k - 1) * (MOE_MAX_PASSES + 1)]

                @pl.when(e_k >= 0)
                def _(k=k, e_k=e_k):
                    for c in weight_copies(e_k, (ws + k) % MOE_W_BUFS):
                        c.start(priority=1)

        ahead = nxt_ref[(MOE_W_AHEAD - 1) * (MOE_MAX_PASSES + 1) + p]

        @pl.when(ahead >= 0)
        def _():
            for c in weight_copies(ahead, (ws + MOE_W_AHEAD) % MOE_W_BUFS):
                c.start(priority=1)

        for c in weight_copies(pe_ref[p], ws):
            c.wait()
        wg_bf[...] = wg_f[ws].astype(BF16)
        wu_bf[...] = wu_f[ws].astype(BF16)
        wd_bf[...] = wd_f[ws].astype(BF16)

    def gather_row(q_base, q_last, r, xb, sem):
        tok = order_ref[q_base + jnp.minimum(r, q_last)] >> 1
        return pltpu.make_async_copy(h_hbm.at[pl.ds(tok, 1), :], xb.at[pl.ds(r, 1), :], sem)

    def pass_span(q):
        return row0_ref[q], jnp.maximum(valid_ref[q], 1) - 1

    def wait_gather(xb, sem):
        pltpu.make_async_copy(h_hbm.at[pl.ds(0, MOE_ROWS), :], xb, sem).wait()

    def scatter_row(q_base, r, yb, sem):
        a = order_ref[q_base + r]
        return pltpu.make_async_copy(yb.at[pl.ds(r, 1), :], out_hbm.at[a & 1, pl.ds(a >> 1, 1), :], sem)

    def issue_scatter(q_base, yb, sem, n):
        def body8(i, c):
            for u in range(MOE_ISSUE_UNROLL):
                scatter_row(q_base, i * MOE_ISSUE_UNROLL + u, yb, sem).start()
            return c

        def body1(r, c):
            scatter_row(q_base, r, yb, sem).start()
            return c

        n8 = n // MOE_ISSUE_UNROLL
        lax.fori_loop(0, n8, body8, 0)
        lax.fori_loop(n8 * MOE_ISSUE_UNROLL, n, body1, 0)

    def wait_scatter(yb, sem, n):
        bit = MOE_ROWS
        while bit >= 1:
            @pl.when((n & bit) != 0)
            def _(bit=bit):
                pltpu.make_async_copy(yb.at[pl.ds(0, bit), :], out_hbm.at[0, pl.ds(0, bit), :], sem).wait()
            bit //= 2

    def run_pass(x_cur, x_nxt, y_cur, y_oth, s_cur, s_nxt):
        @pl.when(first_ref[p] == 1)
        def _():
            load_expert_weights()

        wait_gather(x_cur, sem_in.at[s_cur])
        nxt_base, nxt_last = pass_span(p + 1)
        for r in range(MOE_ROWS):
            gather_row(nxt_base, nxt_last, r, x_nxt, sem_in.at[s_nxt]).start()

        x = x_cur[...].astype(BF16)
        gate = jnp.dot(x, wg_bf[...], preferred_element_type=F32)
        up = jnp.dot(x, wu_bf[...], preferred_element_type=F32)
        hid = (_silu(gate) * up).astype(BF16)
        y = jnp.dot(hid, wd_bf[...], preferred_element_type=F32)

        @pl.when(p >= 2)
        def _():
            wait_scatter(y_cur, sem_out.at[s_cur], valid_ref[jnp.maximum(p - 2, 0)])

        y_cur[...] = y
        issue_scatter(row0_ref[p], y_cur, sem_out.at[s_cur], valid_ref[p])

        @pl.when(p == total - 1)
        def _():
            wait_gather(x_nxt, sem_in.at[s_nxt])

            @pl.when(p >= 1)
            def _():
                wait_scatter(y_oth, sem_out.at[s_nxt], valid_ref[jnp.maximum(p - 1, 0)])

            wait_scatter(y_cur, sem_out.at[s_cur], valid_ref[p])

    @pl.when((p == 0) & (total > 0))
    def _():
        base0, last0 = pass_span(0)

        def body(r, c):
            gather_row(base0, last0, r, xbuf0, sem_in.at[0]).start()
            return c

        lax.fori_loop(0, MOE_ROWS, body, 0, unroll=MOE_ISSUE_UNROLL)

    @pl.when((p < total) & (p % 2 == 0))
    def _():
        run_pass(xbuf0, xbuf1, ybuf0, ybuf1, 0, 1)

    @pl.when((p < total) & (p % 2 == 1))
    def _():
        run_pass(xbuf1, xbuf0, ybuf1, ybuf0, 1, 0)


def _moe_passes(eid):
    i32 = jnp.int32
    order = jnp.argsort(eid).astype(i32)
    count = jnp.sum((eid[:, None] == jnp.arange(N_EXPERTS, dtype=i32)[None, :]).astype(i32), axis=0)
    start = jnp.cumsum(count) - count
    npass = (count + MOE_ROWS - 1) // MOE_ROWS
    pend = jnp.cumsum(npass)
    total = pend[-1]
    last_e = jnp.max(jnp.where(count > 0, jnp.arange(N_EXPERTS, dtype=i32), 0))
    pidx = jnp.arange(MOE_MAX_PASSES + 1, dtype=i32)
    pe = jnp.minimum(jnp.sum((pend[None, :] <= pidx[:, None]).astype(i32), axis=1), last_e)
    k_in = pidx - (pend - npass)[pe]
    valid = jnp.clip(count[pe] - k_in * MOE_ROWS, 0, MOE_ROWS)
    first = ((k_in == 0) & (pidx < total)).astype(i32)
    row0 = jnp.clip(start[pe] + k_in * MOE_ROWS, 0, N_ASSIGN - 1).astype(i32)
    used = count > 0
    ordinal = jnp.cumsum(used.astype(i32)) - 1
    wslot = (ordinal[pe] % MOE_W_BUFS).astype(i32)
    ids = jnp.where(used, jnp.arange(N_EXPERTS, dtype=i32), N_EXPERTS)
    after = jnp.concatenate([ids[1:], jnp.full((1,), N_EXPERTS, i32)])
    step = jnp.concatenate([jnp.flip(lax.cummin(jnp.flip(after))), jnp.full((1,), N_EXPERTS, i32)])
    nxt, cur = [], pe
    for _ in range(MOE_W_AHEAD):
        cur = step[cur]
        nxt.append(jnp.where(cur < N_EXPERTS, cur, -1).astype(i32))
    return pe, first, valid.astype(i32), row0, order, total.reshape(1).astype(i32), wslot, jnp.concatenate(nxt)


def _moe(h2_all, eid, w_gate, w_up, w_down, layer):
    tables = _moe_passes(eid)
    hbm = pl.BlockSpec(memory_space=pl.ANY)
    grid_spec = pltpu.PrefetchScalarGridSpec(
        num_scalar_prefetch=len(tables),
        grid=(MOE_MAX_PASSES,),
        in_specs=[hbm, hbm, hbm, hbm],
        out_specs=hbm,
        scratch_shapes=[
            pltpu.VMEM((MOE_ROWS, D_MODEL), F32), pltpu.VMEM((MOE_ROWS, D_MODEL), F32),
            pltpu.VMEM((MOE_ROWS, D_MODEL), F32), pltpu.VMEM((MOE_ROWS, D_MODEL), F32),
            pltpu.VMEM((MOE_W_BUFS, D_MODEL, D_EXPERT), F32),
            pltpu.VMEM((MOE_W_BUFS, D_MODEL, D_EXPERT), F32),
            pltpu.VMEM((MOE_W_BUFS, D_EXPERT, D_MODEL), F32),
            pltpu.VMEM((D_MODEL, D_EXPERT), BF16),
            pltpu.VMEM((D_MODEL, D_EXPERT), BF16),
            pltpu.VMEM((D_EXPERT, D_MODEL), BF16),
            pltpu.SemaphoreType.DMA((2,)),
            pltpu.SemaphoreType.DMA((2,)),
            pltpu.SemaphoreType.DMA((MOE_W_BUFS,)),
        ],
    )
    return pl.pallas_call(
        functools.partial(_moe_kernel, layer),
        grid_spec=grid_spec,
        out_shape=jax.ShapeDtypeStruct((EXP_TOP_K, T_ALL, D_MODEL), F32),
        compiler_params=_cparams(("arbitrary",)),
        name="moe_experts",
    )(*tables, h2_all, w_gate, w_up, w_down)


def _final_kernel(x1_ref, o0_ref, o1_ref, rt_ref, gf_ref, g_ref, b_ref, y_ref):
    rt = rt_ref[...]
    ffn = rt[:, 2:3] * o0_ref[...] + rt[:, 3:4] * o1_ref[...]
    y_ref[...] = _ln(DN_ALPHA * x1_ref[...] + gf_ref[...] * ffn) * g_ref[...] + b_ref[...]


FINAL_TM = 512


def _final(x1, out2, route, mod, lw, layer, prompt):
    t = x1.shape[0]
    tm = FINAL_TM if prompt else DEC_BATCH
    blk0 = 0 if prompt else T_PROMPT // tm

    def const(shape):
        return pl.BlockSpec((None,) + shape, lambda i: (layer,) + (0,) * len(shape))

    return pl.pallas_call(
        _final_kernel,
        grid=(t // tm,),
        in_specs=[
            pl.BlockSpec((tm, D_MODEL), lambda i: (i, 0)),
            pl.BlockSpec((None, tm, D_MODEL), lambda i: (0, i + blk0, 0)),
            pl.BlockSpec((None, tm, D_MODEL), lambda i: (1, i + blk0, 0)),
            pl.BlockSpec((tm, ROUTE_W), lambda i: (i, 0)),
            _mod_spec(prompt, layer, MOD_G_F, tm),
            const((1, D_MODEL)), const((1, D_MODEL)),
        ],
        out_specs=pl.BlockSpec((tm, D_MODEL), lambda i: (i, 0)),
        out_shape=jax.ShapeDtypeStruct((t, D_MODEL), F32),
        compiler_params=_cparams(("arbitrary",)),
        name="final_p" if prompt else "final_s",
    )(x1, out2, out2, route, mod, lw["ln2_g"], lw["ln2_b"])


def _layer_params(i, conv_w, conv_b, dt_bias, a_log, d_skip, ssd_norm_w, v_ln_g, v_ln_b, w_s, b_s,
                  stacked):
    pad_h = LANES - SSD_HEADS
    cw, cb = conv_w[i], conv_b[i][None, :]
    lw = dict(stacked)
    lw.update(
        cwx=cw[:, :SSD_INNER], cwb=cw[:, SSD_INNER:SSD_INNER + BC_W], cwc=cw[:, SSD_INNER + BC_W:],
        cbx=cb[:, :SSD_INNER], cbb=cb[:, SSD_INNER:SSD_INNER + BC_W], cbc=cb[:, SSD_INNER + BC_W:],
        dtb=jnp.pad(dt_bias[i], (0, pad_h))[None, :],
        alog=jnp.pad(a_log[i], (0, pad_h))[None, :],
        dskip=jnp.repeat(d_skip[i], SSD_HEAD_DIM)[None, :],
        nw=ssd_norm_w[i][None, :],
        vg=v_ln_g[i][None, :], vb=v_ln_b[i][None, :],
        ws=w_s[i], bst=b_s[i].T,
        ws0=jnp.repeat(w_s[i, :, 0, 0], SG_GDIM)[None, :],
        bs0=jnp.repeat(b_s[i, :, 0], SG_GDIM)[None, :],
    )
    return lw


def kernel(x_prompt, x_sample, state_ssd, state_conv, c_prompt, c_sample, w_ada, b_ada, w_in, conv_w, conv_b,
           dt_bias, a_log, d_skip, ssd_norm_w, v_ln_g, v_ln_b, w_s, b_s, w_pa, w_pb, w_o, ln1_g, ln1_b,
           w_rg, b_rg, w_re, b_re, w_gate, w_up, w_down, ln2_g, ln2_b):
    xp = x_prompt.reshape(T_PROMPT, D_MODEL)
    xs = x_sample.reshape(DEC_BATCH, D_MODEL)

    n_c = BATCH + DEC_BATCH
    c_all = jnp.pad(jnp.concatenate([c_sample, c_prompt], axis=0), ((0, (-n_c) % 8), (0, 0)))
    mod_s = _ada_mod(c_all, w_ada, b_ada)
    mod_p = mod_s[:, DEC_BATCH:n_c].reshape(DEPTH, BATCH, 1, 6 * D_MODEL)

    w_in_p = jnp.swapaxes(w_in, 1, 2)
    r_pad = ROUTE_W - N_EXP_GROUPS - N_EXPERTS
    stacked = dict(
        w_pa=w_pa.astype(BF16), w_pb=w_pb.astype(BF16), w_o=w_o.astype(BF16),
        ln1_g=ln1_g[:, None, :], ln1_b=ln1_b[:, None, :], ln2_g=ln2_g[:, None, :], ln2_b=ln2_b[:, None, :],
        w_r=jnp.pad(jnp.concatenate([w_rg, w_re], axis=-1), ((0, 0), (0, 0), (0, r_pad))).astype(BF16),
        b_r=jnp.pad(jnp.concatenate([b_rg, b_re], axis=-1), ((0, 0), (0, r_pad)))[:, None, :],
    )
    conv_state = state_conv.reshape(DEPTH, DEC_BATCH, (CONV_W - 1) * CONV_CH)
    state_all = state_ssd.reshape(DEPTH, DEC_BATCH, SSD_INNER, SSD_STATE)

    ssd_p, conv_p, conv_s, v_s = [], [], [], []
    ssd_s = None
    for i in range(DEPTH):
        lw = _layer_params(i, conv_w, conv_b, dt_bias, a_log, d_skip, ssd_norm_w, v_ln_g, v_ln_b, w_s, b_s,
                           stacked)
        proj_p = _inproj(xp, mod_p, w_in_p, i, True, 1024)
        proj_s = _inproj(xs, mod_s, w_in_p, i, False, DEC_BATCH)
        ya_p, yb_p, st_p, tail_p = _mix_prompt(proj_p, lw, i)
        xa, bm, cm, dec, dtx, yb_s, v_rows, ncs = _sprep(proj_s, conv_state, lw, i)
        ssd_s, ya_s = _sstate(state_all, ssd_s, (xa, bm, cm, dec, dtx), proj_s, lw, i)
        x1_p, h2_all, rt_p = _merge(ya_p, yb_p, proj_p, xp, mod_p, lw, i, True, None)
        x1_s, h2_all, rt_s = _merge(ya_s, yb_s, proj_s, xs, mod_s, lw, i, False, h2_all)
        route = jnp.concatenate([rt_p, rt_s], axis=0)
        eid = route[:, :EXP_TOP_K].astype(jnp.int32).reshape(-1)
        out2 = _moe(h2_all, eid, w_gate, w_up, w_down, i)
        xp = _final(x1_p, out2, rt_p, mod_p, lw, i, True)
        xs = _final(x1_s, out2, rt_s, mod_s, lw, i, False)

        ssd_p.append(st_p.reshape(BATCH, SSD_HEADS, SSD_HEAD_DIM, SSD_STATE))
        conv_p.append(tail_p)
        conv_s.append(ncs.reshape(DEC_BATCH, CONV_W - 1, CONV_CH))
        v_s.append(v_rows.reshape(DEC_BATCH, 1, SG_WIDTH))

    return (xp.reshape(BATCH, SEQ, D_MODEL), xs.reshape(DEC_BATCH, 1, D_MODEL),
            jnp.stack(ssd_p), jnp.stack(conv_p),
            ssd_s.reshape(DEPTH, DEC_BATCH, SSD_HEADS, SSD_HEAD_DIM, SSD_STATE),
            jnp.stack(conv_s), jnp.stack(v_s))
```
